```python
import jax, jax.numpy as jnp
from jax import lax
import numpy as np

D_MODEL = 1024
BATCH = 16
SEQ = 2048
DEPTH = 2
DEC_BATCH = 2
DEC_SEQ = 16384
PAST_LEN = 128

N_META = 16
MIX_WIDTH = D_MODEL
MLA_HEADS = 4
QK_NOPE = 128
QK_ROPE = 64
V_HEAD = 128
MLA_WIDTH = MLA_HEADS * V_HEAD
CONV_WIDTH = MIX_WIDTH - MLA_WIDTH
CONV_K = 3
Q_LORA = 384
KV_LORA = 256
D_FF = -(-8 * D_MODEL // (3 * 256)) * 256
IN_COLS = Q_LORA + KV_LORA + QK_ROPE + 3 * CONV_WIDTH
ROPE_THETA = 10000.0
EPS = 1e-6
Q_BLOCK = 128
ATTN_SCALE = (QK_NOPE + QK_ROPE) ** -0.5

kernel_name = "hymba_mla_shortconv_encoder"


def rmsnorm(x, g):
    xf = x.astype(jnp.float32)
    y = xf * lax.rsqrt(jnp.mean(xf * xf, axis=-1, keepdims=True) + EPS)
    return (y * g.astype(jnp.float32)).astype(x.dtype)


def rope_tables(length, dtype):
    inv_freq = ROPE_THETA ** (-jnp.arange(0, QK_ROPE, 2, dtype=jnp.float32) / QK_ROPE)
    ang = jnp.arange(length, dtype=jnp.float32)[:, None] * inv_freq[None, :]
    return jnp.cos(ang).astype(dtype), jnp.sin(ang).astype(dtype)


def apply_rope(x, cos, sin):
    half = QK_ROPE // 2
    x1, x2 = x[..., :half], x[..., half:]
    return jnp.concatenate([x1 * cos - x2 * sin, x2 * cos + x1 * sin], axis=-1)


def mla_attention(q_nope, q_rope, k_nope, k_rope, v):
    b, l, h, _ = q_nope.shape

    def attend(qn, qr):
        s = (jnp.einsum('bqhd,bkhd->bhqk', qn, k_nope)
             + jnp.einsum('bqhr,bkr->bhqk', qr, k_rope))
        p = jax.nn.softmax(s.astype(jnp.float32) * ATTN_SCALE, axis=-1).astype(v.dtype)
        return jnp.einsum('bhqk,bkhd->bqhd', p, v)

    out_meta = attend(q_nope[:, :N_META], q_rope[:, :N_META])
    n_blk = (l - N_META) // Q_BLOCK
    qn = jnp.moveaxis(q_nope[:, N_META:].reshape(b, n_blk, Q_BLOCK, h, QK_NOPE), 1, 0)
    qr = jnp.moveaxis(q_rope[:, N_META:].reshape(b, n_blk, Q_BLOCK, h, QK_ROPE), 1, 0)
    out_real = lax.map(lambda a: attend(a[0], a[1]), (qn, qr))
    out_real = jnp.moveaxis(out_real, 0, 1).reshape(b, l - N_META, h, V_HEAD)
    out = jnp.concatenate([out_meta, out_real], axis=1)
    return out.reshape(b, l, h * V_HEAD)


def short_conv(u, w):
    l = u.shape[1]
    pad = CONV_K // 2
    up = jnp.pad(u, ((0, 0), (pad, pad), (0, 0)))
    y = up[:, 0:l] * w[0]
    for k in range(1, CONV_K):
        y = y + up[:, k:k + l] * w[k]
    return y


def layer(x, pre_mix_g, w_in, q_norm_g, w_q_up, kv_norm_g, w_kv_up, conv_w, w_out,
          post_mix_g, pre_ffn_g, w_gate, w_up, w_down, post_ffn_g):
    b, l, _ = x.shape
    h = rmsnorm(x, pre_mix_g)
    z = h @ w_in
    i0 = Q_LORA
    i1 = i0 + KV_LORA
    i2 = i1 + QK_ROPE
    i3 = i2 + CONV_WIDTH
    i4 = i3 + CONV_WIDTH
    c_q, c_kv, k_r = z[..., :i0], z[..., i0:i1], z[..., i1:i2]
    g_b, g_c, c_h = z[..., i2:i3], z[..., i3:i4], z[..., i4:]

    q = (rmsnorm(c_q, q_norm_g) @ w_q_up).reshape(b, l, MLA_HEADS, QK_NOPE + QK_ROPE)
    kv = (rmsnorm(c_kv, kv_norm_g) @ w_kv_up).reshape(b, l, MLA_HEADS, QK_NOPE + V_HEAD)
    cos, sin = rope_tables(l, x.dtype)
    q_nope = q[..., :QK_NOPE]
    q_rope = apply_rope(q[..., QK_NOPE:], cos[None, :, None, :], sin[None, :, None, :])
    k_nope, v = kv[..., :QK_NOPE], kv[..., QK_NOPE:]
    k_rope = apply_rope(k_r, cos[None], sin[None])
    attn_out = mla_attention(q_nope, q_rope, k_nope, k_rope, v)

    conv_out = g_b * short_conv(g_c * c_h, conv_w)

    mix = jnp.concatenate([attn_out, conv_out], axis=-1) @ w_out
    x = x + rmsnorm(mix, post_mix_g)

    h = rmsnorm(x, pre_ffn_g)
    f = (jax.nn.silu(h @ w_gate) * (h @ w_up)) @ w_down
    return x + rmsnorm(f, post_ffn_g)


def trunk(x, meta_tokens, pre_mix_g, w_in, q_norm_g, w_q_up, kv_norm_g, w_kv_up, conv_w,
          w_out, post_mix_g, pre_ffn_g, w_gate, w_up, w_down, post_ffn_g):
    b = x.shape[0]
    meta = jnp.broadcast_to(meta_tokens.astype(x.dtype)[None], (b, N_META, D_MODEL))
    x = jnp.concatenate([meta, x], axis=1)
    for i in range(DEPTH):
        x = layer(x, pre_mix_g[i], w_in[i], q_norm_g[i], w_q_up[i], kv_norm_g[i], w_kv_up[i],
                  conv_w[i], w_out[i], post_mix_g[i], pre_ffn_g[i], w_gate[i], w_up[i],
                  w_down[i], post_ffn_g[i])
    return x[:, N_META:]


def setup_inputs(seed: int = 0) -> dict:
    key = jax.random.key(seed)
    ks = jax.random.split(key, 20)
    f32 = jnp.float32

    def nrm(k, shape, scale):
        return jax.random.normal(k, shape, f32) * scale

    def gain(k, shape):
        return 1.0 + 0.05 * jax.random.normal(k, shape, f32)

    return {
        "x_prompt": nrm(ks[0], (BATCH, SEQ, D_MODEL), 1.0),
        "x_sample": nrm(ks[1], (DEC_BATCH, DEC_SEQ, D_MODEL), 1.0),
        "meta_tokens": nrm(ks[2], (N_META, D_MODEL), 1.0),
        "pre_mix_g": gain(ks[3], (DEPTH, D_MODEL)),
        "w_in": nrm(ks[4], (DEPTH, D_MODEL, IN_COLS), D_MODEL ** -0.5),
        "q_norm_g": gain(ks[5], (DEPTH, Q_LORA)),
        "w_q_up": nrm(ks[6], (DEPTH, Q_LORA, MLA_HEADS * (QK_NOPE + QK_ROPE)), Q_LORA ** -0.5),
        "kv_norm_g": gain(ks[7], (DEPTH, KV_LORA)),
        "w_kv_up": nrm(ks[8], (DEPTH, KV_LORA, MLA_HEADS * (QK_NOPE + V_HEAD)), KV_LORA ** -0.5),
        "conv_w": nrm(ks[9], (DEPTH, CONV_K, CONV_WIDTH), CONV_K ** -0.5),
        "w_out": nrm(ks[10], (DEPTH, MIX_WIDTH, D_MODEL), MIX_WIDTH ** -0.5),
        "post_mix_g": gain(ks[11], (DEPTH, D_MODEL)),
        "pre_ffn_g": gain(ks[12], (DEPTH, D_MODEL)),
        "w_gate": nrm(ks[13], (DEPTH, D_MODEL, D_FF), D_MODEL ** -0.5),
        "w_up": nrm(ks[14], (DEPTH, D_MODEL, D_FF), D_MODEL ** -0.5),
        "w_down": nrm(ks[15], (DEPTH, D_FF, D_MODEL), D_FF ** -0.5),
        "post_ffn_g": gain(ks[16], (DEPTH, D_MODEL)),
    }


def reference(x_prompt, x_sample, meta_tokens, pre_mix_g, w_in, q_norm_g, w_q_up, kv_norm_g,
              w_kv_up, conv_w, w_out, post_mix_g, pre_ffn_g, w_gate, w_up, w_down, post_ffn_g):
    y_prompt = trunk(x_prompt, meta_tokens, pre_mix_g, w_in, q_norm_g, w_q_up, kv_norm_g,
                     w_kv_up, conv_w, w_out, post_mix_g, pre_ffn_g, w_gate, w_up, w_down,
                     post_ffn_g)
    y_sample = trunk(x_sample, meta_tokens, pre_mix_g, w_in, q_norm_g, w_q_up, kv_norm_g,
                     w_kv_up, conv_w, w_out, post_mix_g, pre_ffn_g, w_gate, w_up, w_down,
                     post_ffn_g)
    return (y_prompt, y_sample)
```

```python
import functools

import jax
import jax.numpy as jnp
from jax import lax
from jax.experimental import pallas as pl
from jax.experimental.pallas import tpu as pltpu

F32 = jnp.float32
BF16 = jnp.bfloat16

D_MODEL = 1024
DEPTH = 2
N_META = 16
HEADS = 4
QK_NOPE = 128
QK_ROPE = 64
ROPE_HALF = QK_ROPE // 2
V_HEAD = 128
MLA_WIDTH = HEADS * V_HEAD
CONV_WIDTH = D_MODEL - MLA_WIDTH
CONV_K = 3
Q_LORA = 384
KV_LORA = 256
D_FF = 2816
ROPE_THETA = 10000.0
EPS = 1e-6
ATTN_SCALE = (QK_NOPE + QK_ROPE) ** -0.5

LANES = 128
BF16_SUBLANES = 16
MXU_DIM = 256
VMEM_LIMIT = 56 * 1024 * 1024

QK_PAD = QK_NOPE + LANES
META_PAD = LANES
ROW_TILE = 512
Q_CHUNK = 512
K_CHUNK = 512
META_ROWS = 384
FF_CHUNKS = ((0, 1280), (1280, 2816))

C_Q0, C_KV0, C_KA0, C_KB0 = 0, Q_LORA, Q_LORA + KV_LORA, Q_LORA + KV_LORA + LANES
C_GB0 = C_KB0 + LANES
C_GC0 = C_GB0 + CONV_WIDTH
C_CH0 = C_GC0 + CONV_WIDTH
IN_COLS_P = C_CH0 + CONV_WIDTH
Q_ROWS_P = HEADS * (QK_NOPE + 2 * LANES)

NEG_BIG = -1e30


def _rms(x, g):
    ms = jnp.mean(x * x, axis=-1, keepdims=True)
    return x * lax.rsqrt(ms + EPS) * g


def _dot(a, b):
    return jnp.dot(a, b, preferred_element_type=F32)


def _dot_nt(a, b):
    return lax.dot_general(a, b, (((1,), (1,)), ((), ())), preferred_element_type=F32)


def _proj_body(x, g_pre, w_in_ref, g_q, wq_t_ref, g_kv, wk_ref, wv_t_ref, cos, sin, cos_t, sin_t):
    h = _rms(x, g_pre).astype(BF16)
    z = _dot(h, w_in_ref[...])
    cqn = _rms(z[:, C_Q0:C_KV0], g_q).astype(BF16)
    ckvn = _rms(z[:, C_KV0:C_KA0], g_kv).astype(BF16)
    k_rot = z[:, C_KA0:C_KB0] * cos + z[:, C_KB0:C_GB0] * sin
    g_b = z[:, C_GB0:C_GC0]
    u = z[:, C_GC0:C_CH0] * z[:, C_CH0:IN_COLS_P]
    q_t = _dot_nt(wq_t_ref[...], cqn)
    k_nope = _dot(ckvn, wk_ref[...])
    v_t = _dot_nt(wv_t_ref[...], ckvn)
    q_heads = []
    for hd in range(HEADS):
        r0 = hd * (QK_NOPE + 2 * LANES)
        nope = q_t[r0:r0 + QK_NOPE]
        rot = (q_t[r0 + QK_NOPE:r0 + QK_NOPE + LANES] * cos_t
               + q_t[r0 + QK_NOPE + LANES:r0 + QK_NOPE + 2 * LANES] * sin_t)
        q_heads.append(((nope * ATTN_SCALE).astype(BF16), (rot * ATTN_SCALE).astype(BF16)))
    return q_heads, k_nope.astype(BF16), k_rot.astype(BF16), v_t.astype(BF16), g_b.astype(BF16), u.astype(BF16)


def _proj_kernel(x_ref, g_pre_ref, w_in_ref, g_q_ref, wq_t_ref, g_kv_ref, wk_ref, wv_t_ref,
                 cos_ref, sin_ref, cos_t_ref, sin_t_ref,
                 q_ref, k_ref, v_ref, gb_ref, u_ref):
    q_heads, k_nope, k_rot, v_t, g_b, u = _proj_body(
        x_ref[0], g_pre_ref[...], w_in_ref, g_q_ref[...], wq_t_ref, g_kv_ref[...], wk_ref,
        wv_t_ref, cos_ref[...], sin_ref[...], cos_t_ref[...], sin_t_ref[...])
    n_q = ROW_TILE // Q_CHUNK
    n_k = ROW_TILE // K_CHUNK
    for hd in range(HEADS):
        nope, rot = q_heads[hd]
        for c in range(n_q):
            q_ref[0, hd, c, :QK_NOPE, :] = nope[:, c * Q_CHUNK:(c + 1) * Q_CHUNK]
            q_ref[0, hd, c, QK_NOPE:, :] = rot[:, c * Q_CHUNK:(c + 1) * Q_CHUNK]
        k_ref[0, hd, :, :QK_NOPE] = k_nope[:, hd * QK_NOPE:(hd + 1) * QK_NOPE]
        k_ref[0, hd, :, QK_NOPE:] = k_rot
        for c in range(n_k):
            v_ref[0, hd, c] = v_t[hd * V_HEAD:(hd + 1) * V_HEAD, c * K_CHUNK:(c + 1) * K_CHUNK]
    gb_ref[0] = g_b
    u_ref[0] = u


def _proj_meta_kernel(x_ref, g_pre_ref, w_in_ref, g_q_ref, wq_t_ref, g_kv_ref, wk_ref, wv_t_ref,
                      cos_ref, sin_ref, cos_t_ref, sin_t_ref,
                      q_ref, k_ref, v_ref, gb_ref, u_ref):
    q_heads, k_nope, k_rot, v_t, g_b, u = _proj_body(
        x_ref[...], g_pre_ref[...], w_in_ref, g_q_ref[...], wq_t_ref, g_kv_ref[...], wk_ref,
        wv_t_ref, cos_ref[...], sin_ref[...], cos_t_ref[...], sin_t_ref[...])
    for hd in range(HEADS):
        nope, rot = q_heads[hd]
        q_ref[hd, :QK_NOPE, :] = nope
        q_ref[hd, QK_NOPE:, :] = rot
        k_ref[hd, :, :QK_NOPE] = k_nope[:, hd * QK_NOPE:(hd + 1) * QK_NOPE]
        k_ref[hd, :, QK_NOPE:] = k_rot
        v_ref[hd] = v_t[hd * V_HEAD:(hd + 1) * V_HEAD, :]
    gb_ref[...] = g_b
    u_ref[...] = u


def _const_spec(shape):
    zeros = (0,) * len(shape)
    return pl.BlockSpec(shape, lambda *_: zeros, pipeline_mode=pl.Buffered(1))


def _proj_weight_specs():
    return [
        _const_spec((1, D_MODEL)),
        _const_spec((D_MODEL, IN_COLS_P)),
        _const_spec((1, Q_LORA)),
        _const_spec((Q_ROWS_P, Q_LORA)),
        _const_spec((1, KV_LORA)),
        _const_spec((KV_LORA, HEADS * QK_NOPE)),
        _const_spec((HEADS * V_HEAD, KV_LORA)),
    ]


def _proj_real(x, lw, tabs):
    b, seq, _ = x.shape
    nt = seq // ROW_TILE
    cos, sin, cos_t, sin_t = tabs
    in_specs = ([pl.BlockSpec((1, ROW_TILE, D_MODEL), lambda i, j: (i, j, 0))]
                + _proj_weight_specs()
                + [pl.BlockSpec((ROW_TILE, LANES), lambda i, j: (j, 0)),
                   pl.BlockSpec((ROW_TILE, LANES), lambda i, j: (j, 0)),
                   pl.BlockSpec((LANES, ROW_TILE), lambda i, j: (0, j)),
                   pl.BlockSpec((LANES, ROW_TILE), lambda i, j: (0, j))])
    n_q = ROW_TILE // Q_CHUNK
    n_k = ROW_TILE // K_CHUNK
    out_shape = [
        jax.ShapeDtypeStruct((b, HEADS, seq // Q_CHUNK, QK_PAD, Q_CHUNK), BF16),
        jax.ShapeDtypeStruct((b, HEADS, seq, QK_PAD), BF16),
        jax.ShapeDtypeStruct((b, HEADS, seq // K_CHUNK, V_HEAD, K_CHUNK), BF16),
        jax.ShapeDtypeStruct((b, seq, CONV_WIDTH), BF16),
        jax.ShapeDtypeStruct((b, seq, CONV_WIDTH), BF16),
    ]
    out_specs = [
        pl.BlockSpec((1, HEADS, n_q, QK_PAD, Q_CHUNK), lambda i, j: (i, 0, j, 0, 0)),
        pl.BlockSpec((1, HEADS, ROW_TILE, QK_PAD), lambda i, j: (i, 0, j, 0)),
        pl.BlockSpec((1, HEADS, n_k, V_HEAD, K_CHUNK), lambda i, j: (i, 0, j, 0, 0)),
        pl.BlockSpec((1, ROW_TILE, CONV_WIDTH), lambda i, j: (i, j, 0)),
        pl.BlockSpec((1, ROW_TILE, CONV_WIDTH), lambda i, j: (i, j, 0)),
    ]
    return pl.pallas_call(
        _proj_kernel,
        grid=(b, nt),
        in_specs=in_specs,
        out_specs=out_specs,
        out_shape=out_shape,
        compiler_params=pltpu.CompilerParams(
            dimension_semantics=("parallel", "parallel"), vmem_limit_bytes=VMEM_LIMIT),
        name="proj_real",
    )(x, lw["g_pre"], lw["w_in"], lw["g_q"], lw["wq_t"], lw["g_kv"], lw["wk"], lw["wv_t"],
      cos, sin, cos_t, sin_t)


def _proj_meta(xm, lw, tabs):
    cos, sin, cos_t, sin_t = tabs
    r = META_ROWS
    in_specs = ([_const_spec((r, D_MODEL))] + _proj_weight_specs()
                + [_const_spec((r, LANES)), _const_spec((r, LANES)),
                   _const_spec((LANES, r)), _const_spec((LANES, r))])
    out_shape = [
        jax.ShapeDtypeStruct((HEADS, QK_PAD, r), BF16),
        jax.ShapeDtypeStruct((HEADS, r, QK_PAD), BF16),
        jax.ShapeDtypeStruct((HEADS, V_HEAD, r), BF16),
        jax.ShapeDtypeStruct((r, CONV_WIDTH), BF16),
        jax.ShapeDtypeStruct((r, CONV_WIDTH), BF16),
    ]
    out_specs = [
        pl.BlockSpec((HEADS, QK_PAD, r), lambda i: (0, 0, 0)),
        pl.BlockSpec((HEADS, r, QK_PAD), lambda i: (0, 0, 0)),
        pl.BlockSpec((HEADS, V_HEAD, r), lambda i: (0, 0, 0)),
        pl.BlockSpec((r, CONV_WIDTH), lambda i: (0, 0)),
        pl.BlockSpec((r, CONV_WIDTH), lambda i: (0, 0)),
    ]
    return pl.pallas_call(
        _proj_meta_kernel,
        grid=(1,),
        in_specs=in_specs,
        out_specs=out_specs,
        out_shape=out_shape,
        compiler_params=pltpu.CompilerParams(
            dimension_semantics=("arbitrary",), vmem_limit_bytes=VMEM_LIMIT),
        name="proj_meta",
    )(xm, lw["g_pre"], lw["w_in"], lw["g_q"], lw["wq_t"], lw["g_kv"], lw["wk"], lw["wv_t"],
      cos, sin, cos_t, sin_t)


def _attn_kernel(q_ref, k_ref, v_ref, km_ref, vm_ref, o_ref, *, n_qc, n_kc, q_chunk):
    meta_valid = lax.broadcasted_iota(jnp.int32, (META_PAD, q_chunk), 0) < N_META

    def q_step(qc, _):
        q_t = q_ref[0, 0, qc]
        s0 = jnp.where(meta_valid, _dot(km_ref[0, 0], q_t), NEG_BIG)
        m0 = jnp.max(s0, axis=0, keepdims=True)
        p0 = jnp.exp(s0 - m0)
        l0 = jnp.sum(p0, axis=0, keepdims=True)
        acc0 = _dot(vm_ref[0, 0], p0.astype(BF16))

        def k_step(kc, carry):
            m, l, acc = carry
            k = k_ref[0, 0, pl.ds(pl.multiple_of(kc * K_CHUNK, K_CHUNK), K_CHUNK), :]
            s = _dot(k, q_t)
            m_new = jnp.maximum(m, jnp.max(s, axis=0, keepdims=True))
            alpha = jnp.exp(m - m_new)
            p = jnp.exp(s - m_new)
            l_new = alpha * l + jnp.sum(p, axis=0, keepdims=True)
            acc_new = alpha * acc + _dot(v_ref[0, 0, kc], p.astype(BF16))
            return m_new, l_new, acc_new

        _, l, acc = lax.fori_loop(0, n_kc, k_step, (m0, l0, acc0))
        out = (acc / l).T
        o_ref[0, pl.ds(pl.multiple_of(qc * q_chunk, q_chunk), q_chunk), :] = out.astype(o_ref.dtype)
        return 0

    lax.fori_loop(0, n_qc, q_step, 0)


def _attention(q_t, k, v_t, k_meta, v_meta_t, *, q_chunks_per_step):
    b, _, n_q_total, _, q_chunk = q_t.shape
    seq_k = k.shape[2]
    n_kc = seq_k // K_CHUNK
    n_qc = q_chunks_per_step
    n_steps = n_q_total // n_qc
    tq = n_qc * q_chunk
    kern = functools.partial(_attn_kernel, n_qc=n_qc, n_kc=n_kc, q_chunk=q_chunk)
    return pl.pallas_call(
        kern,
        grid=(b, HEADS, n_steps),
        in_specs=[
            pl.BlockSpec((1, 1, n_qc, QK_PAD, q_chunk), lambda i, h, j: (i, h, j, 0, 0)),
            pl.BlockSpec((1, 1, seq_k, QK_PAD), lambda i, h, j: (i, h, 0, 0)),
            pl.BlockSpec((1, 1, n_kc, V_HEAD, K_CHUNK), lambda i, h, j: (i, h, 0, 0, 0)),
            pl.BlockSpec((1, 1, META_PAD, QK_PAD), lambda i, h, j: (i, h, 0, 0)),
            pl.BlockSpec((1, 1, V_HEAD, META_PAD), lambda i, h, j: (i, h, 0, 0)),
        ],
        out_specs=pl.BlockSpec((1, tq, V_HEAD), lambda i, h, j: (i, j, h)),
        out_shape=jax.ShapeDtypeStruct((b, n_q_total * q_chunk, MLA_WIDTH), BF16),
        compiler_params=pltpu.CompilerParams(
            dimension_semantics=("parallel", "parallel", "parallel"), vmem_limit_bytes=VMEM_LIMIT),
        name="attention",
    )(q_t, k, v_t, k_meta, v_meta_t)


def _mix_ffn_body(x, attn, g_b, u_prev, u_cur, u_next, conv_w, w_out_ref, g_post_mix, g_pre_ffn,
                  w_gate_ref, w_up_ref, w_down_ref, g_post_ffn):
    y = u_prev * conv_w[0:1] + u_cur * conv_w[1:2] + u_next * conv_w[2:3]
    conv = (g_b.astype(F32) * y).astype(BF16)
    mix = _dot(attn, w_out_ref[:MLA_WIDTH, :]) + _dot(conv, w_out_ref[MLA_WIDTH:, :])
    x1 = x + _rms(mix, g_post_mix)
    h = _rms(x1, g_pre_ffn).astype(BF16)
    f = None
    for c0, c1 in FF_CHUNKS:
        gate = _dot(h, w_gate_ref[:, c0:c1])
        up = _dot(h, w_up_ref[:, c0:c1])
        act = (gate / (1.0 + jnp.exp(-gate)) * up).astype(BF16)
        part = _dot(act, w_down_ref[c0:c1, :])
        f = part if f is None else f + part
    return x1 + _rms(f, g_post_ffn)


def _mix_ffn_kernel(x_ref, attn_ref, gb_ref, u_ref, u_before_ref, u_after_ref, u_meta_ref,
                    conv_w_ref, w_out_ref, g_post_mix_ref, g_pre_ffn_ref,
                    w_gate_ref, w_up_ref, w_down_ref, g_post_ffn_ref, o_ref):
    j = pl.program_id(1)
    last = pl.num_programs(1) - 1
    u_cur = u_ref[0].astype(F32)
    tail = BF16_SUBLANES - 1
    before = jnp.where(j == 0, u_meta_ref[0, tail:tail + 1, :], u_before_ref[0, tail:tail + 1, :])
    after = jnp.where(j == last, jnp.zeros_like(u_after_ref[0, 0:1, :]), u_after_ref[0, 0:1, :])
    row = lax.broadcasted_iota(jnp.int32, u_cur.shape, 0)
    u_prev = jnp.where(row == 0, before.astype(F32), pltpu.roll(u_cur, 1, 0))
    u_next = jnp.where(row == ROW_TILE - 1, after.astype(F32), pltpu.roll(u_cur, ROW_TILE - 1, 0))
    o_ref[0] = _mix_ffn_body(
        x_ref[0], attn_ref[0], gb_ref[0], u_prev, u_cur, u_next, conv_w_ref[...], w_out_ref,
        g_post_mix_ref[...], g_pre_ffn_ref[...], w_gate_ref, w_up_ref, w_down_ref,
        g_post_ffn_ref[...])


def _mix_ffn_meta_kernel(x_ref, attn_ref, gb_ref, u_prev_ref, u_ref, u_next_ref,
                         conv_w_ref, w_out_ref, g_post_mix_ref, g_pre_ffn_ref,
                         w_gate_ref, w_up_ref, w_down_ref, g_post_ffn_ref, o_ref):
    o_ref[...] = _mix_ffn_body(
        x_ref[...], attn_ref[...], gb_ref[...], u_prev_ref[...].astype(F32),
        u_ref[...].astype(F32), u_next_ref[...].astype(F32), conv_w_ref[...], w_out_ref,
        g_post_mix_ref[...], g_pre_ffn_ref[...], w_gate_ref, w_up_ref, w_down_ref,
        g_post_ffn_ref[...])


def _ffn_weight_specs():
    return [
        _const_spec((CONV_K, CONV_WIDTH)),
        _const_spec((D_MODEL, D_MODEL)),
        _const_spec((1, D_MODEL)),
        _const_spec((1, D_MODEL)),
        _const_spec((D_MODEL, D_FF)),
        _const_spec((D_MODEL, D_FF)),
        _const_spec((D_FF, D_MODEL)),
        _const_spec((1, D_MODEL)),
    ]


def _ffn_weights(lw):
    return (lw["conv_w"], lw["w_out"], lw["g_post_mix"], lw["g_pre_ffn"], lw["w_gate"], lw["w_up"],
            lw["w_down"], lw["g_post_ffn"])


def _mix_ffn_real(x, attn, g_b, u, u_meta, lw):
    b, seq, _ = x.shape
    nt = seq // ROW_TILE
    halo_per_tile = ROW_TILE // BF16_SUBLANES
    n_halo = seq // BF16_SUBLANES
    tile = lambda w: pl.BlockSpec((1, ROW_TILE, w), lambda i, j: (i, j, 0))
    in_specs = [
        tile(D_MODEL), tile(MLA_WIDTH), tile(CONV_WIDTH), tile(CONV_WIDTH),
        pl.BlockSpec((1, BF16_SUBLANES, CONV_WIDTH),
                     lambda i, j: (i, jnp.maximum(j * halo_per_tile - 1, 0), 0)),
        pl.BlockSpec((1, BF16_SUBLANES, CONV_WIDTH),
                     lambda i, j: (i, jnp.minimum((j + 1) * halo_per_tile, n_halo - 1), 0)),
        pl.BlockSpec((1, BF16_SUBLANES, CONV_WIDTH), lambda i, j: (i, 0, 0)),
    ] + _ffn_weight_specs()
    return pl.pallas_call(
        _mix_ffn_kernel,
        grid=(b, nt),
        in_specs=in_specs,
        out_specs=tile(D_MODEL),
        out_shape=jax.ShapeDtypeStruct((b, seq, D_MODEL), F32),
        compiler_params=pltpu.CompilerParams(
            dimension_semantics=("parallel", "parallel"), vmem_limit_bytes=VMEM_LIMIT),
        name="mix_ffn_real",
    )(x, attn, g_b, u, u, u, u_meta, *_ffn_weights(lw))


def _mix_ffn_meta(xm, attn, g_b, u_prev, u, u_next, lw):
    r = META_ROWS
    in_specs = [_const_spec((r, D_MODEL)), _const_spec((r, MLA_WIDTH))] \
        + [_const_spec((r, CONV_WIDTH)) for _ in range(4)] + _ffn_weight_specs()
    return pl.pallas_call(
        _mix_ffn_meta_kernel,
        grid=(1,),
        in_specs=in_specs,
        out_specs=pl.BlockSpec((r, D_MODEL), lambda i: (0, 0)),
        out_shape=jax.ShapeDtypeStruct((r, D_MODEL), F32),
        compiler_params=pltpu.CompilerParams(
            dimension_semantics=("arbitrary",), vmem_limit_bytes=VMEM_LIMIT),
        name="mix_ffn_meta",
    )(xm, attn, g_b, u_prev, u, u_next, *_ffn_weights(lw))


def _pack_layer(i, pre_mix_g, w_in, q_norm_g, w_q_up, kv_norm_g, w_kv_up, conv_w, w_out,
                post_mix_g, pre_ffn_g, w_gate, w_up, w_down, post_ffn_g):
    wi = w_in[i]
    i1 = Q_LORA + KV_LORA
    i2 = i1 + QK_ROPE
    k_r = wi[:, i1:i2]
    zpad = jnp.zeros((D_MODEL, LANES - QK_ROPE), F32)
    k_a = jnp.concatenate([k_r, zpad], axis=1)
    k_b = jnp.concatenate([k_r[:, ROPE_HALF:], k_r[:, :ROPE_HALF], zpad], axis=1)
    w_in_p = jnp.concatenate([wi[:, :i1], k_a, k_b, wi[:, i2:]], axis=1).astype(BF16)

    wq = w_q_up[i].reshape(Q_LORA, HEADS, QK_NOPE + QK_ROPE)
    qz = jnp.zeros((Q_LORA, HEADS, LANES - QK_ROPE), F32)
    q_rope = wq[:, :, QK_NOPE:]
    wq_p = jnp.concatenate(
        [wq[:, :, :QK_NOPE], q_rope, qz,
         q_rope[:, :, ROPE_HALF:], q_rope[:, :, :ROPE_HALF], qz], axis=2)
    wq_t = wq_p.reshape(Q_LORA, Q_ROWS_P).T.astype(BF16)

    wkv = w_kv_up[i].reshape(KV_LORA, HEADS, QK_NOPE + V_HEAD)
    wk = wkv[:, :, :QK_NOPE].reshape(KV_LORA, HEADS * QK_NOPE).astype(BF16)
    wv_t = wkv[:, :, QK_NOPE:].reshape(KV_LORA, HEADS * V_HEAD).T.astype(BF16)

    row = lambda g: g[i].reshape(1, -1).astype(F32)
    return dict(
        g_pre=row(pre_mix_g), w_in=w_in_p, g_q=row(q_norm_g), wq_t=wq_t, g_kv=row(kv_norm_g),
        wk=wk, wv_t=wv_t, conv_w=conv_w[i].astype(F32), w_out=w_out[i].astype(BF16),
        g_post_mix=row(post_mix_g), g_pre_ffn=row(pre_ffn_g), w_gate=w_gate[i].astype(BF16),
        w_up=w_up[i].astype(BF16), w_down=w_down[i].astype(BF16), g_post_ffn=row(post_ffn_g))


def _rope_tables(pos):
    inv_freq = ROPE_THETA ** (-jnp.arange(0, QK_ROPE, 2, dtype=F32) / QK_ROPE)
    ang = pos.astype(F32)[:, None] * inv_freq[None, :]
    c, s = jnp.cos(ang), jnp.sin(ang)
    z = jnp.zeros((pos.shape[0], LANES - QK_ROPE), F32)
    cos = jnp.concatenate([c, c, z], axis=1)
    sin = jnp.concatenate([-s, s, z], axis=1)
    return cos, sin, cos.T, sin.T


def _meta_to_groups(a, batches, axis):
    out, r0 = [], 0
    for b in batches:
        sl = lax.slice_in_dim(a, r0, r0 + b * N_META, axis=axis)
        shape = a.shape[:axis] + (b, N_META) + a.shape[axis + 1:]
        out.append(sl.reshape(shape))
        r0 += b * N_META
    return out


def _join_meta(parts):
    flat = [p.reshape(-1, p.shape[-1]) for p in parts]
    rows = sum(f.shape[0] for f in flat)
    flat.append(jnp.zeros((META_ROWS - rows, flat[0].shape[-1]), flat[0].dtype))
    return jnp.concatenate(flat, axis=0)


def kernel(x_prompt, x_sample, meta_tokens, pre_mix_g, w_in, q_norm_g, w_q_up, kv_norm_g, w_kv_up,
           conv_w, w_out, post_mix_g, pre_ffn_g, w_gate, w_up, w_down, post_ffn_g):
    xs = [x_prompt, x_sample]
    batches = [x.shape[0] for x in xs]
    assert sum(batches) * N_META <= META_ROWS
    for x in xs:
        assert x.shape[1] % ROW_TILE == 0 and x.shape[2] == D_MODEL

    real_tabs = [_rope_tables(N_META + jnp.arange(x.shape[1])) for x in xs]
    meta_pos = jnp.arange(META_ROWS) % N_META
    meta_tabs = _rope_tables(meta_pos)

    xm = _join_meta([jnp.broadcast_to(meta_tokens.astype(F32)[None], (b, N_META, D_MODEL))
                     for b in batches])

    for layer in range(DEPTH):
        lw = _pack_layer(layer, pre_mix_g, w_in, q_norm_g, w_q_up, kv_norm_g, w_kv_up, conv_w,
                         w_out, post_mix_g, pre_ffn_g, w_gate, w_up, w_down, post_ffn_g)
        qm, km, vm, gbm, um = _proj_meta(xm, lw, meta_tabs)
        qm_g = _meta_to_groups(qm, batches, 2)
        km_g = _meta_to_groups(km, batches, 1)
        vm_g = _meta_to_groups(vm, batches, 2)
        um_g = _meta_to_groups(um, batches, 0)

        new_xs, attn_meta, u_prev_meta, u_next_meta = [], [], [], []
        for g, x in enumerate(xs):
            q_t, k, v_t, g_b, u = _proj_real(x, lw, real_tabs[g])
            k_meta = jnp.pad(jnp.transpose(km_g[g], (1, 0, 2, 3)),
                             ((0, 0), (0, 0), (0, META_PAD - N_META), (0, 0)))
            v_meta_t = jnp.pad(jnp.transpose(vm_g[g], (2, 0, 1, 3)),
                               ((0, 0), (0, 0), (0, 0), (0, META_PAD - N_META)))
            q_meta_t = jnp.pad(jnp.transpose(qm_g[g], (2, 0, 1, 3)),
                               ((0, 0), (0, 0), (0, 0), (0, META_PAD - N_META)))[:, :, None]
            n_q_total = x.shape[1] // Q_CHUNK
            attn = _attention(q_t, k, v_t, k_meta, v_meta_t,
                              q_chunks_per_step=min(n_q_total, 4))
            attn_m = _attention(q_meta_t, k, v_t, k_meta, v_meta_t, q_chunks_per_step=1)
            attn_meta.append(attn_m[:, :N_META])
            um_b = um_g[g]
            zero_row = jnp.zeros_like(um_b[:, :1])
            u_prev_meta.append(jnp.concatenate([zero_row, um_b[:, :-1]], axis=1))
            u_next_meta.append(jnp.concatenate([um_b[:, 1:], u[:, :1]], axis=1))
            new_xs.append(_mix_ffn_real(x, attn, g_b, u, um_b, lw))

        xm = _mix_ffn_meta(xm, _join_meta(attn_meta), gbm, _join_meta(u_prev_meta), um,
                           _join_meta(u_next_meta), lw)
        xs = new_xs

    return (xs[0], xs[1])
```

```python
import functools

import jax
import jax.numpy as jnp
from jax import lax
from jax.experimental import pallas as pl
from jax.experimental.pallas import tpu as pltpu

F32 = jnp.float32
BF16 = jnp.bfloat16

D_MODEL = 1024
DEPTH = 2
N_META = 16
HEADS = 4
QK_NOPE = 128
QK_ROPE = 64
ROPE_HALF = QK_ROPE // 2
V_HEAD = 128
MLA_WIDTH = HEADS * V_HEAD
CONV_WIDTH = D_MODEL - MLA_WIDTH
CONV_K = 3
Q_LORA = 384
KV_LORA = 256
D_FF = 2816
ROPE_THETA = 10000.0
EPS = 1e-6
ATTN_SCALE = (QK_NOPE + QK_ROPE) ** -0.5
LOG2_E = 1.4426950408889634
Q_SCALE = ATTN_SCALE * LOG2_E

LANES = 128
BF16_SUBLANES = 16
MXU_DIM = 256
VMEM_LIMIT = 56 * 1024 * 1024

QK_PAD = QK_NOPE + LANES
META_PAD = LANES
ROW_TILE = 512
Q_CHUNK = 512
K_CHUNK = 512
K_UNROLL = 4
META_ROWS = 384
FF_CHUNKS = ((0, 1280), (1280, 2816))

C_Q0, C_KV0, C_KA0, C_KB0 = 0, Q_LORA, Q_LORA + KV_LORA, Q_LORA + KV_LORA + LANES
C_GB0 = C_KB0 + LANES
C_GC0 = C_GB0 + CONV_WIDTH
C_CH0 = C_GC0 + CONV_WIDTH
IN_COLS_P = C_CH0 + CONV_WIDTH
Q_ROWS_P = HEADS * (QK_NOPE + 2 * LANES)

NEG_BIG = -1e30


def _rms(x, g):
    ms = jnp.mean(x * x, axis=-1, keepdims=True)
    return x * lax.rsqrt(ms + EPS) * g


def _dot(a, b):
    return jnp.dot(a, b, preferred_element_type=F32)


def _dot_nt(a, b):
    return lax.dot_general(a, b, (((1,), (1,)), ((), ())), preferred_element_type=F32)


def _proj_body(x, g_pre, w_in_ref, g_q, wq_t_ref, g_kv, wk_ref, wv_t_ref, cos, sin, cos_t, sin_t):
    h = _rms(x, g_pre).astype(BF16)
    z = _dot(h, w_in_ref[...])
    cqn = _rms(z[:, C_Q0:C_KV0], g_q).astype(BF16)
    ckvn = _rms(z[:, C_KV0:C_KA0], g_kv).astype(BF16)
    k_rot = z[:, C_KA0:C_KB0] * cos + z[:, C_KB0:C_GB0] * sin
    g_b = z[:, C_GB0:C_GC0]
    u = z[:, C_GC0:C_CH0] * z[:, C_CH0:IN_COLS_P]
    q_t = _dot_nt(wq_t_ref[...], cqn)
    k_nope = _dot(ckvn, wk_ref[...])
    v_t = _dot_nt(wv_t_ref[...], ckvn)
    q_heads = []
    for hd in range(HEADS):
        r0 = hd * (QK_NOPE + 2 * LANES)
        nope = q_t[r0:r0 + QK_NOPE]
        rot = (q_t[r0 + QK_NOPE:r0 + QK_NOPE + LANES] * cos_t
               + q_t[r0 + QK_NOPE + LANES:r0 + QK_NOPE + 2 * LANES] * sin_t)
        q_heads.append(((nope * Q_SCALE).astype(BF16), (rot * Q_SCALE).astype(BF16)))
    return q_heads, k_nope.astype(BF16), k_rot.astype(BF16), v_t.astype(BF16), g_b.astype(BF16), u.astype(BF16)


def _proj_kernel(x_ref, g_pre_ref, w_in_ref, g_q_ref, wq_t_ref, g_kv_ref, wk_ref, wv_t_ref,
                 cos_ref, sin_ref, cos_t_ref, sin_t_ref,
                 q_ref, k_ref, v_ref, gb_ref, u_ref):
    q_heads, k_nope, k_rot, v_t, g_b, u = _proj_body(
        x_ref[0], g_pre_ref[...], w_in_ref, g_q_ref[...], wq_t_ref, g_kv_ref[...], wk_ref,
        wv_t_ref, cos_ref[...], sin_ref[...], cos_t_ref[...], sin_t_ref[...])
    n_q = ROW_TILE // Q_CHUNK
    n_k = ROW_TILE // K_CHUNK
    for hd in range(HEADS):
        nope, rot = q_heads[hd]
        for c in range(n_q):
            q_ref[0, hd, c, :QK_NOPE, :] = nope[:, c * Q_CHUNK:(c + 1) * Q_CHUNK]
            q_ref[0, hd, c, QK_NOPE:, :] = rot[:, c * Q_CHUNK:(c + 1) * Q_CHUNK]
        k_ref[0, hd, :, :QK_NOPE] = k_nope[:, hd * QK_NOPE:(hd + 1) * QK_NOPE]
        k_ref[0, hd, :, QK_NOPE:] = k_rot
        for c in range(n_k):
            v_ref[0, hd, c] = v_t[hd * V_HEAD:(hd + 1) * V_HEAD, c * K_CHUNK:(c + 1) * K_CHUNK]
    gb_ref[0] = g_b
    u_ref[0] = u


def _proj_meta_kernel(x_ref, g_pre_ref, w_in_ref, g_q_ref, wq_t_ref, g_kv_ref, wk_ref, wv_t_ref,
                      cos_ref, sin_ref, cos_t_ref, sin_t_ref,
                      q_ref, k_ref, v_ref, gb_ref, u_ref):
    q_heads, k_nope, k_rot, v_t, g_b, u = _proj_body(
        x_ref[...], g_pre_ref[...], w_in_ref, g_q_ref[...], wq_t_ref, g_kv_ref[...], wk_ref,
        wv_t_ref, cos_ref[...], sin_ref[...], cos_t_ref[...], sin_t_ref[...])
    for hd in range(HEADS):
        nope, rot = q_heads[hd]
        q_ref[hd, :QK_NOPE, :] = nope
        q_ref[hd, QK_NOPE:, :] = rot
        k_ref[hd, :, :QK_NOPE] = k_nope[:, hd * QK_NOPE:(hd + 1) * QK_NOPE]
        k_ref[hd, :, QK_NOPE:] = k_rot
        v_ref[hd] = v_t[hd * V_HEAD:(hd + 1) * V_HEAD, :]
    gb_ref[...] = g_b
    u_ref[...] = u


def _const_spec(shape):
    zeros = (0,) * len(shape)
    return pl.BlockSpec(shape, lambda *_: zeros, pipeline_mode=pl.Buffered(1))


def _proj_weight_specs():
    return [
        _const_spec((1, D_MODEL)),
        _const_spec((D_MODEL, IN_COLS_P)),
        _const_spec((1, Q_LORA)),
        _const_spec((Q_ROWS_P, Q_LORA)),
        _const_spec((1, KV_LORA)),
        _const_spec((KV_LORA, HEADS * QK_NOPE)),
        _const_spec((HEADS * V_HEAD, KV_LORA)),
    ]


def _proj_real(x, lw, tabs):
    b, seq, _ = x.shape
    nt = seq // ROW_TILE
    cos, sin, cos_t, sin_t = tabs
    in_specs = ([pl.BlockSpec((1, ROW_TILE, D_MODEL), lambda i, j: (i, j, 0))]
                + _proj_weight_specs()
                + [pl.BlockSpec((ROW_TILE, LANES), lambda i, j: (j, 0)),
                   pl.BlockSpec((ROW_TILE, LANES), lambda i, j: (j, 0)),
                   pl.BlockSpec((LANES, ROW_TILE), lambda i, j: (0, j)),
                   pl.BlockSpec((LANES, ROW_TILE), lambda i, j: (0, j))])
    n_q = ROW_TILE // Q_CHUNK
    n_k = ROW_TILE // K_CHUNK
    out_shape = [
        jax.ShapeDtypeStruct((b, HEADS, seq // Q_CHUNK, QK_PAD, Q_CHUNK), BF16),
        jax.ShapeDtypeStruct((b, HEADS, seq, QK_PAD), BF16),
        jax.ShapeDtypeStruct((b, HEADS, seq // K_CHUNK, V_HEAD, K_CHUNK), BF16),
        jax.ShapeDtypeStruct((b, seq, CONV_WIDTH), BF16),
        jax.ShapeDtypeStruct((b, seq, CONV_WIDTH), BF16),
    ]
    out_specs = [
        pl.BlockSpec((1, HEADS, n_q, QK_PAD, Q_CHUNK), lambda i, j: (i, 0, j, 0, 0)),
        pl.BlockSpec((1, HEADS, ROW_TILE, QK_PAD), lambda i, j: (i, 0, j, 0)),
        pl.BlockSpec((1, HEADS, n_k, V_HEAD, K_CHUNK), lambda i, j: (i, 0, j, 0, 0)),
        pl.BlockSpec((1, ROW_TILE, CONV_WIDTH), lambda i, j: (i, j, 0)),
        pl.BlockSpec((1, ROW_TILE, CONV_WIDTH), lambda i, j: (i, j, 0)),
    ]
    return pl.pallas_call(
        _proj_kernel,
        grid=(b, nt),
        in_specs=in_specs,
        out_specs=out_specs,
        out_shape=out_shape,
        compiler_params=pltpu.CompilerParams(
            dimension_semantics=("parallel", "parallel"), vmem_limit_bytes=VMEM_LIMIT),
        name="proj_real",
    )(x, lw["g_pre"], lw["w_in"], lw["g_q"], lw["wq_t"], lw["g_kv"], lw["wk"], lw["wv_t"],
      cos, sin, cos_t, sin_t)


def _proj_meta(xm, lw, tabs):
    cos, sin, cos_t, sin_t = tabs
    r = META_ROWS
    in_specs = ([_const_spec((r, D_MODEL))] + _proj_weight_specs()
                + [_const_spec((r, LANES)), _const_spec((r, LANES)),
                   _const_spec((LANES, r)), _const_spec((LANES, r))])
    out_shape = [
        jax.ShapeDtypeStruct((HEADS, QK_PAD, r), BF16),
        jax.ShapeDtypeStruct((HEADS, r, QK_PAD), BF16),
        jax.ShapeDtypeStruct((HEADS, V_HEAD, r), BF16),
        jax.ShapeDtypeStruct((r, CONV_WIDTH), BF16),
        jax.ShapeDtypeStruct((r, CONV_WIDTH), BF16),
    ]
    out_specs = [
        pl.BlockSpec((HEADS, QK_PAD, r), lambda i: (0, 0, 0)),
        pl.BlockSpec((HEADS, r, QK_PAD), lambda i: (0, 0, 0)),
        pl.BlockSpec((HEADS, V_HEAD, r), lambda i: (0, 0, 0)),
        pl.BlockSpec((r, CONV_WIDTH), lambda i: (0, 0)),
        pl.BlockSpec((r, CONV_WIDTH), lambda i: (0, 0)),
    ]
    return pl.pallas_call(
        _proj_meta_kernel,
        grid=(1,),
        in_specs=in_specs,
        out_specs=out_specs,
        out_shape=out_shape,
        compiler_params=pltpu.CompilerParams(
            dimension_semantics=("arbitrary",), vmem_limit_bytes=VMEM_LIMIT),
        name="proj_meta",
    )(xm, lw["g_pre"], lw["w_in"], lw["g_q"], lw["wq_t"], lw["g_kv"], lw["wk"], lw["wv_t"],
      cos, sin, cos_t, sin_t)


def _attn_kernel(q_ref, k_ref, v_ref, km_ref, vm_ref, o_ref, s_ref, *, n_qc, n_kc, q_chunk, unroll):
    meta_valid = lax.broadcasted_iota(jnp.int32, (META_PAD, q_chunk), 0) < N_META

    def ones_rows(n):
        first = lax.broadcasted_iota(jnp.int32, (BF16_SUBLANES, n), 0) == 0
        return jnp.where(first, 1.0, 0.0).astype(BF16)

    def put_scores(slot, kc, qc):
        k = k_ref[0, 0, pl.ds(pl.multiple_of(kc * K_CHUNK, K_CHUNK), K_CHUNK), :]
        s_ref[slot] = _dot(k, q_ref[0, 0, qc])

    def with_ones(v_t):
        return jnp.concatenate([v_t, ones_rows(v_t.shape[1])], axis=0)

    def online_update(s, v_t, carry):
        m, acc = carry
        m_new = jnp.maximum(m, jnp.max(s, axis=0, keepdims=True))
        alpha = jnp.exp2(m - m_new)
        p = jnp.exp2(s - m_new)
        return m_new, alpha * acc + _dot(with_ones(v_t), p.astype(BF16))

    def q_step(qc, _):
        def k_group(base, carry, last):
            for t in range(unroll):
                if t + 1 < unroll or not last:
                    put_scores((t + 1) % 2, base + t + 1, qc)
                else:
                    put_scores(0, 0, jnp.minimum(qc + 1, n_qc - 1))
                carry = online_update(s_ref[t % 2], v_ref[0, 0, base + t], carry)
                if last and t == 0:
                    s_meta = jnp.where(meta_valid, _dot(km_ref[0, 0], q_ref[0, 0, qc]), NEG_BIG)
                    carry = online_update(s_meta, vm_ref[0, 0], carry)
            return carry

        m0 = jnp.full((1, q_chunk), NEG_BIG, F32)
        acc0 = jnp.zeros((V_HEAD + BF16_SUBLANES, q_chunk), F32)
        n_groups = n_kc // unroll
        carry = lax.fori_loop(0, n_groups - 1, lambda g, c: k_group(g * unroll, c, False),
                              (m0, acc0))
        _, acc = k_group((n_groups - 1) * unroll, carry, True)
        out = (acc[:V_HEAD] / acc[V_HEAD:V_HEAD + 1]).T
        o_ref[0, pl.ds(pl.multiple_of(qc * q_chunk, q_chunk), q_chunk), :] = out.astype(o_ref.dtype)
        return 0

    put_scores(0, 0, 0)
    lax.fori_loop(0, n_qc, q_step, 0)


def _attention(q_t, k, v_t, k_meta, v_meta_t, *, q_chunks_per_step):
    b, _, n_q_total, _, q_chunk = q_t.shape
    seq_k = k.shape[2]
    n_kc = seq_k // K_CHUNK
    unroll = min(K_UNROLL, n_kc)
    assert unroll % 2 == 0 and n_kc % unroll == 0
    n_qc = q_chunks_per_step
    n_steps = n_q_total // n_qc
    tq = n_qc * q_chunk
    kern = functools.partial(_attn_kernel, n_qc=n_qc, n_kc=n_kc, q_chunk=q_chunk, unroll=unroll)
    return pl.pallas_call(
        kern,
        grid=(b, HEADS, n_steps),
        in_specs=[
            pl.BlockSpec((1, 1, n_qc, QK_PAD, q_chunk), lambda i, h, j: (i, h, j, 0, 0)),
            pl.BlockSpec((1, 1, seq_k, QK_PAD), lambda i, h, j: (i, h, 0, 0)),
            pl.BlockSpec((1, 1, n_kc, V_HEAD, K_CHUNK), lambda i, h, j: (i, h, 0, 0, 0)),
            pl.BlockSpec((1, 1, META_PAD, QK_PAD), lambda i, h, j: (i, h, 0, 0)),
            pl.BlockSpec((1, 1, V_HEAD, META_PAD), lambda i, h, j: (i, h, 0, 0)),
        ],
        out_specs=pl.BlockSpec((1, tq, V_HEAD), lambda i, h, j: (i, j, h)),
        out_shape=jax.ShapeDtypeStruct((b, n_q_total * q_chunk, MLA_WIDTH), BF16),
        scratch_shapes=[pltpu.VMEM((2, K_CHUNK, q_chunk), F32)],
        compiler_params=pltpu.CompilerParams(
            dimension_semantics=("parallel", "parallel", "parallel"), vmem_limit_bytes=VMEM_LIMIT),
        name="attention",
    )(q_t, k, v_t, k_meta, v_meta_t)


def _mix_ffn_body(x, attn, g_b, u_prev, u_cur, u_next, conv_w, w_out_ref, g_post_mix, g_pre_ffn,
                  w_gate_ref, w_up_ref, w_down_ref, g_post_ffn):
    y = u_prev * conv_w[0:1] + u_cur * conv_w[1:2] + u_next * conv_w[2:3]
    conv = (g_b.astype(F32) * y).astype(BF16)
    mix = _dot(attn, w_out_ref[:MLA_WIDTH, :]) + _dot(conv, w_out_ref[MLA_WIDTH:, :])
    x1 = x + _rms(mix, g_post_mix)
    h = _rms(x1, g_pre_ffn).astype(BF16)
    f = None
    for c0, c1 in FF_CHUNKS:
        gate = _dot(h, w_gate_ref[:, c0:c1])
        up = _dot(h, w_up_ref[:, c0:c1])
        act = (gate / (1.0 + jnp.exp(-gate)) * up).astype(BF16)
        part = _dot(act, w_down_ref[c0:c1, :])
        f = part if f is None else f + part
    return x1 + _rms(f, g_post_ffn)


def _mix_ffn_kernel(x_ref, attn_ref, gb_ref, u_ref, u_before_ref, u_after_ref, u_meta_ref,
                    conv_w_ref, w_out_ref, g_post_mix_ref, g_pre_ffn_ref,
                    w_gate_ref, w_up_ref, w_down_ref, g_post_ffn_ref, o_ref):
    j = pl.program_id(1)
    last = pl.num_programs(1) - 1
    u_cur = u_ref[0].astype(F32)
    tail = BF16_SUBLANES - 1
    before = jnp.where(j == 0, u_meta_ref[0, tail:tail + 1, :], u_before_ref[0, tail:tail + 1, :])
    after = jnp.where(j == last, jnp.zeros_like(u_after_ref[0, 0:1, :]), u_after_ref[0, 0:1, :])
    row = lax.broadcasted_iota(jnp.int32, u_cur.shape, 0)
    u_prev = jnp.where(row == 0, before.astype(F32), pltpu.roll(u_cur, 1, 0))
    u_next = jnp.where(row == ROW_TILE - 1, after.astype(F32), pltpu.roll(u_cur, ROW_TILE - 1, 0))
    o_ref[0] = _mix_ffn_body(
        x_ref[0], attn_ref[0], gb_ref[0], u_prev, u_cur, u_next, conv_w_ref[...], w_out_ref,
        g_post_mix_ref[...], g_pre_ffn_ref[...], w_gate_ref, w_up_ref, w_down_ref,
        g_post_ffn_ref[...])


def _mix_ffn_meta_kernel(x_ref, attn_ref, gb_ref, u_prev_ref, u_ref, u_next_ref,
                         conv_w_ref, w_out_ref, g_post_mix_ref, g_pre_ffn_ref,
                         w_gate_ref, w_up_ref, w_down_ref, g_post_ffn_ref, o_ref):
    o_ref[...] = _mix_ffn_body(
        x_ref[...], attn_ref[...], gb_ref[...], u_prev_ref[...].astype(F32),
        u_ref[...].astype(F32), u_next_ref[...].astype(F32), conv_w_ref[...], w_out_ref,
        g_post_mix_ref[...], g_pre_ffn_ref[...], w_gate_ref, w_up_ref, w_down_ref,
        g_post_ffn_ref[...])


def _ffn_weight_specs():
    return [
        _const_spec((CONV_K, CONV_WIDTH)),
        _const_spec((D_MODEL, D_MODEL)),
        _const_spec((1, D_MODEL)),
        _const_spec((1, D_MODEL)),
        _const_spec((D_MODEL, D_FF)),
        _const_spec((D_MODEL, D_FF)),
        _const_spec((D_FF, D_MODEL)),
        _const_spec((1, D_MODEL)),
    ]


def _ffn_weights(lw):
    return (lw["conv_w"], lw["w_out"], lw["g_post_mix"], lw["g_pre_ffn"], lw["w_gate"], lw["w_up"],
            lw["w_down"], lw["g_post_ffn"])


def _mix_ffn_real(x, attn, g_b, u, u_meta, lw):
    b, seq, _ = x.shape
    nt = seq // ROW_TILE
    halo_per_tile = ROW_TILE // BF16_SUBLANES
    n_halo = seq // BF16_SUBLANES
    tile = lambda w: pl.BlockSpec((1, ROW_TILE, w), lambda i, j: (i, j, 0))
    in_specs = [
        tile(D_MODEL), tile(MLA_WIDTH), tile(CONV_WIDTH), tile(CONV_WIDTH),
        pl.BlockSpec((1, BF16_SUBLANES, CONV_WIDTH),
                     lambda i, j: (i, jnp.maximum(j * halo_per_tile - 1, 0), 0)),
        pl.BlockSpec((1, BF16_SUBLANES, CONV_WIDTH),
                     lambda i, j: (i, jnp.minimum((j + 1) * halo_per_tile, n_halo - 1), 0)),
        pl.BlockSpec((1, BF16_SUBLANES, CONV_WIDTH), lambda i, j: (i, 0, 0)),
    ] + _ffn_weight_specs()
    return pl.pallas_call(
        _mix_ffn_kernel,
        grid=(b, nt),
        in_specs=in_specs,
        out_specs=tile(D_MODEL),
        out_shape=jax.ShapeDtypeStruct((b, seq, D_MODEL), F32),
        compiler_params=pltpu.CompilerParams(
            dimension_semantics=("parallel", "parallel"), vmem_limit_bytes=VMEM_LIMIT),
        name="mix_ffn_real",
    )(x, attn, g_b, u, u, u, u_meta, *_ffn_weights(lw))


def _mix_ffn_meta(xm, attn, g_b, u_prev, u, u_next, lw):
    r = META_ROWS
    in_specs = [_const_spec((r, D_MODEL)), _const_spec((r, MLA_WIDTH))] \
        + [_const_spec((r, CONV_WIDTH)) for _ in range(4)] + _ffn_weight_specs()
    return pl.pallas_call(
        _mix_ffn_meta_kernel,
        grid=(1,),
        in_specs=in_specs,
        out_specs=pl.BlockSpec((r, D_MODEL), lambda i: (0, 0)),
        out_shape=jax.ShapeDtypeStruct((r, D_MODEL), F32),
        compiler_params=pltpu.CompilerParams(
            dimension_semantics=("arbitrary",), vmem_limit_bytes=VMEM_LIMIT),
        name="mix_ffn_meta",
    )(xm, attn, g_b, u_prev, u, u_next, *_ffn_weights(lw))


def _pack_layer(i, pre_mix_g, w_in, q_norm_g, w_q_up, kv_norm_g, w_kv_up, conv_w, w_out,
                post_mix_g, pre_ffn_g, w_gate, w_up, w_down, post_ffn_g):
    wi = w_in[i]
    i1 = Q_LORA + KV_LORA
    i2 = i1 + QK_ROPE
    k_r = wi[:, i1:i2]
    zpad = jnp.zeros((D_MODEL, LANES - QK_ROPE), F32)
    k_a = jnp.concatenate([k_r, zpad], axis=1)
    k_b = jnp.concatenate([k_r[:, ROPE_HALF:], k_r[:, :ROPE_HALF], zpad], axis=1)
    w_in_p = jnp.concatenate([wi[:, :i1], k_a, k_b, wi[:, i2:]], axis=1).astype(BF16)

    wq = w_q_up[i].reshape(Q_LORA, HEADS, QK_NOPE + QK_ROPE)
    qz = jnp.zeros((Q_LORA, HEADS, LANES - QK_ROPE), F32)
    q_rope = wq[:, :, QK_NOPE:]
    wq_p = jnp.concatenate(
        [wq[:, :, :QK_NOPE], q_rope, qz,
         q_rope[:, :, ROPE_HALF:], q_rope[:, :, :ROPE_HALF], qz], axis=2)
    wq_t = wq_p.reshape(Q_LORA, Q_ROWS_P).T.astype(BF16)

    wkv = w_kv_up[i].reshape(KV_LORA, HEADS, QK_NOPE + V_HEAD)
    wk = wkv[:, :, :QK_NOPE].reshape(KV_LORA, HEADS * QK_NOPE).astype(BF16)
    wv_t = wkv[:, :, QK_NOPE:].reshape(KV_LORA, HEADS * V_HEAD).T.astype(BF16)

    row = lambda g: g[i].reshape(1, -1).astype(F32)
    return dict(
        g_pre=row(pre_mix_g), w_in=w_in_p, g_q=row(q_norm_g), wq_t=wq_t, g_kv=row(kv_norm_g),
        wk=wk, wv_t=wv_t, conv_w=conv_w[i].astype(F32), w_out=w_out[i].astype(BF16),
        g_post_mix=row(post_mix_g), g_pre_ffn=row(pre_ffn_g), w_gate=w_gate[i].astype(BF16),
        w_up=w_up[i].astype(BF16), w_down=w_down[i].astype(BF16), g_post_ffn=row(post_ffn_g))


def _rope_tables(pos):
    inv_freq = ROPE_THETA ** (-jnp.arange(0, QK_ROPE, 2, dtype=F32) / QK_ROPE)
    ang = pos.astype(F32)[:, None] * inv_freq[None, :]
    c, s = jnp.cos(ang), jnp.sin(ang)
    z = jnp.zeros((pos.shape[0], LANES - QK_ROPE), F32)
    cos = jnp.concatenate([c, c, z], axis=1)
    sin = jnp.concatenate([-s, s, z], axis=1)
    return cos, sin, cos.T, sin.T


def _meta_to_groups(a, batches, axis):
    out, r0 = [], 0
    for b in batches:
        sl = lax.slice_in_dim(a, r0, r0 + b * N_META, axis=axis)
        shape = a.shape[:axis] + (b, N_META) + a.shape[axis + 1:]
        out.append(sl.reshape(shape))
        r0 += b * N_META
    return out


def _join_meta(parts):
    flat = [p.reshape(-1, p.shape[-1]) for p in parts]
    rows = sum(f.shape[0] for f in flat)
    flat.append(jnp.zeros((META_ROWS - rows, flat[0].shape[-1]), flat[0].dtype))
    return jnp.concatenate(flat, axis=0)


def kernel(x_prompt, x_sample, meta_tokens, pre_mix_g, w_in, q_norm_g, w_q_up, kv_norm_g, w_kv_up,
           conv_w, w_out, post_mix_g, pre_ffn_g, w_gate, w_up, w_down, post_ffn_g):
    xs = [x_prompt, x_sample]
    batches = [x.shape[0] for x in xs]
    assert sum(batches) * N_META <= META_ROWS
    for x in xs:
        assert x.shape[1] % ROW_TILE == 0 and x.shape[2] == D_MODEL

    real_tabs = [_rope_tables(N_META + jnp.arange(x.shape[1])) for x in xs]
    meta_pos = jnp.arange(META_ROWS) % N_META
    meta_tabs = _rope_tables(meta_pos)

    xm = _join_meta([jnp.broadcast_to(meta_tokens.astype(F32)[None], (b, N_META, D_MODEL))
                     for b in batches])

    for layer in range(DEPTH):
        lw = _pack_layer(layer, pre_mix_g, w_in, q_norm_g, w_q_up, kv_norm_g, w_kv_up, conv_w,
                         w_out, post_mix_g, pre_ffn_g, w_gate, w_up, w_down, post_ffn_g)
        qm, km, vm, gbm, um = _proj_meta(xm, lw, meta_tabs)
        qm_g = _meta_to_groups(qm, batches, 2)
        km_g = _meta_to_groups(km, batches, 1)
        vm_g = _meta_to_groups(vm, batches, 2)
        um_g = _meta_to_groups(um, batches, 0)

        new_xs, attn_meta, u_prev_meta, u_next_meta = [], [], [], []
        for g, x in enumerate(xs):
            q_t, k, v_t, g_b, u = _proj_real(x, lw, real_tabs[g])
            k_meta = jnp.pad(jnp.transpose(km_g[g], (1, 0, 2, 3)),
                             ((0, 0), (0, 0), (0, META_PAD - N_META), (0, 0)))
            v_meta_t = jnp.pad(jnp.transpose(vm_g[g], (2, 0, 1, 3)),
                               ((0, 0), (0, 0), (0, 0), (0, META_PAD - N_META)))
            q_meta_t = jnp.pad(jnp.transpose(qm_g[g], (2, 0, 1, 3)),
                               ((0, 0), (0, 0), (0, 0), (0, META_PAD - N_META)))[:, :, None]
            n_q_total = x.shape[1] // Q_CHUNK
            attn = _attention(q_t, k, v_t, k_meta, v_meta_t,
                              q_chunks_per_step=min(n_q_total, 4))
            attn_m = _attention(q_meta_t, k, v_t, k_meta, v_meta_t, q_chunks_per_step=1)
            attn_meta.append(attn_m[:, :N_META])
            um_b = um_g[g]
            zero_row = jnp.zeros_like(um_b[:, :1])
            u_prev_meta.append(jnp.concatenate([zero_row, um_b[:, :-1]], axis=1))
            u_next_meta.append(jnp.concatenate([um_b[:, 1:], u[:, :1]], axis=1))
            new_xs.append(_mix_ffn_real(x, attn, g_b, u, um_b, lw))

        xm = _mix_ffn_meta(xm, _join_meta(attn_meta), gbm, _join_meta(u_prev_meta), um,
                           _join_meta(u_next_meta), lw)
        xs = new_xs

    return (xs[0], xs[1])
```

```python
import functools

import jax
import jax.numpy as jnp
from jax import lax
from jax.experimental import pallas as pl
from jax.experimental.pallas import tpu as pltpu

F32 = jnp.float32
BF16 = jnp.bfloat16

D_MODEL = 1024
DEPTH = 2
N_META = 16
HEADS = 4
QK_NOPE = 128
QK_ROPE = 64
ROPE_HALF = QK_ROPE // 2
V_HEAD = 128
MLA_WIDTH = HEADS * V_HEAD
CONV_WIDTH = D_MODEL - MLA_WIDTH
CONV_K = 3
Q_LORA = 384
KV_LORA = 256
D_FF = 2816
ROPE_THETA = 10000.0
EPS = 1e-6
ATTN_SCALE = (QK_NOPE + QK_ROPE) ** -0.5
LOG2_E = 1.4426950408889634
Q_SCALE = ATTN_SCALE * LOG2_E

LANES = 128
BF16_SUBLANES = 16
MXU_DIM = 256
VMEM_LIMIT = 56 * 1024 * 1024

QK_PAD = QK_NOPE + LANES
META_PAD = LANES
ROW_TILE = 512
Q_CHUNK = 512
K_CHUNK = 512
K_UNROLL = 8
META_ROWS = 384
FF_CHUNKS = ((0, 1280), (1280, 2816))

C_Q0, C_KV0, C_KA0, C_KB0 = 0, Q_LORA, Q_LORA + KV_LORA, Q_LORA + KV_LORA + LANES
C_GB0 = C_KB0 + LANES
C_GC0 = C_GB0 + CONV_WIDTH
C_CH0 = C_GC0 + CONV_WIDTH
IN_COLS_P = C_CH0 + CONV_WIDTH
Q_ROWS_P = HEADS * (QK_NOPE + 2 * LANES)

NEG_BIG = -1e30


def _rms(x, g):
    ms = jnp.mean(x * x, axis=-1, keepdims=True)
    return x * lax.rsqrt(ms + EPS) * g


def _dot(a, b):
    return jnp.dot(a, b, preferred_element_type=F32)


def _dot_nt(a, b):
    return lax.dot_general(a, b, (((1,), (1,)), ((), ())), preferred_element_type=F32)


def _proj_body(x, g_pre, w_in_ref, g_q, wq_t_ref, g_kv, wk_ref, wv_t_ref, cos, sin, cos_t, sin_t):
    h = _rms(x, g_pre).astype(BF16)
    z = _dot(h, w_in_ref[...])
    cqn = _rms(z[:, C_Q0:C_KV0], g_q).astype(BF16)
    ckvn = _rms(z[:, C_KV0:C_KA0], g_kv).astype(BF16)
    k_rot = z[:, C_KA0:C_KB0] * cos + z[:, C_KB0:C_GB0] * sin
    g_b = z[:, C_GB0:C_GC0]
    u = z[:, C_GC0:C_CH0] * z[:, C_CH0:IN_COLS_P]
    q_t = _dot_nt(wq_t_ref[...], cqn)
    k_nope = _dot(ckvn, wk_ref[...])
    v_t = _dot_nt(wv_t_ref[...], ckvn)
    q_heads = []
    for hd in range(HEADS):
        r0 = hd * (QK_NOPE + 2 * LANES)
        nope = q_t[r0:r0 + QK_NOPE]
        rot = (q_t[r0 + QK_NOPE:r0 + QK_NOPE + LANES] * cos_t
               + q_t[r0 + QK_NOPE + LANES:r0 + QK_NOPE + 2 * LANES] * sin_t)
        q_heads.append(((nope * Q_SCALE).astype(BF16), (rot * Q_SCALE).astype(BF16)))
    return q_heads, k_nope.astype(BF16), k_rot.astype(BF16), v_t.astype(BF16), g_b.astype(BF16), u.astype(BF16)


def _proj_kernel(x_ref, g_pre_ref, w_in_ref, g_q_ref, wq_t_ref, g_kv_ref, wk_ref, wv_t_ref,
                 cos_ref, sin_ref, cos_t_ref, sin_t_ref,
                 q_ref, k_ref, v_ref, gb_ref, u_ref):
    q_heads, k_nope, k_rot, v_t, g_b, u = _proj_body(
        x_ref[0], g_pre_ref[...], w_in_ref, g_q_ref[...], wq_t_ref, g_kv_ref[...], wk_ref,
        wv_t_ref, cos_ref[...], sin_ref[...], cos_t_ref[...], sin_t_ref[...])
    n_q = ROW_TILE // Q_CHUNK
    n_k = ROW_TILE // K_CHUNK
    for hd in range(HEADS):
        nope, rot = q_heads[hd]
        for c in range(n_q):
            q_ref[0, hd, c, :QK_NOPE, :] = nope[:, c * Q_CHUNK:(c + 1) * Q_CHUNK]
            q_ref[0, hd, c, QK_NOPE:, :] = rot[:, c * Q_CHUNK:(c + 1) * Q_CHUNK]
        k_ref[0, hd, :, :QK_NOPE] = k_nope[:, hd * QK_NOPE:(hd + 1) * QK_NOPE]
        k_ref[0, hd, :, QK_NOPE:] = k_rot
        for c in range(n_k):
            v_ref[0, hd, c] = v_t[hd * V_HEAD:(hd + 1) * V_HEAD, c * K_CHUNK:(c + 1) * K_CHUNK]
    gb_ref[0] = g_b
    u_ref[0] = u


def _proj_meta_kernel(x_ref, g_pre_ref, w_in_ref, g_q_ref, wq_t_ref, g_kv_ref, wk_ref, wv_t_ref,
                      cos_ref, sin_ref, cos_t_ref, sin_t_ref,
                      q_ref, k_ref, v_ref, gb_ref, u_ref):
    q_heads, k_nope, k_rot, v_t, g_b, u = _proj_body(
        x_ref[...], g_pre_ref[...], w_in_ref, g_q_ref[...], wq_t_ref, g_kv_ref[...], wk_ref,
        wv_t_ref, cos_ref[...], sin_ref[...], cos_t_ref[...], sin_t_ref[...])
    for hd in range(HEADS):
        nope, rot = q_heads[hd]
        q_ref[hd, :QK_NOPE, :] = nope
        q_ref[hd, QK_NOPE:, :] = rot
        k_ref[hd, :, :QK_NOPE] = k_nope[:, hd * QK_NOPE:(hd + 1) * QK_NOPE]
        k_ref[hd, :, QK_NOPE:] = k_rot
        v_ref[hd] = v_t[hd * V_HEAD:(hd + 1) * V_HEAD, :]
    gb_ref[...] = g_b
    u_ref[...] = u


def _const_spec(shape):
    zeros = (0,) * len(shape)
    return pl.BlockSpec(shape, lambda *_: zeros, pipeline_mode=pl.Buffered(1))


def _proj_weight_specs():
    return [
        _const_spec((1, D_MODEL)),
        _const_spec((D_MODEL, IN_COLS_P)),
        _const_spec((1, Q_LORA)),
        _const_spec((Q_ROWS_P, Q_LORA)),
        _const_spec((1, KV_LORA)),
        _const_spec((KV_LORA, HEADS * QK_NOPE)),
        _const_spec((HEADS * V_HEAD, KV_LORA)),
    ]


def _proj_real(x, lw, tabs):
    b, seq, _ = x.shape
    nt = seq // ROW_TILE
    cos, sin, cos_t, sin_t = tabs
    in_specs = ([pl.BlockSpec((1, ROW_TILE, D_MODEL), lambda i, j: (i, j, 0))]
                + _proj_weight_specs()
                + [pl.BlockSpec((ROW_TILE, LANES), lambda i, j: (j, 0)),
                   pl.BlockSpec((ROW_TILE, LANES), lambda i, j: (j, 0)),
                   pl.BlockSpec((LANES, ROW_TILE), lambda i, j: (0, j)),
                   pl.BlockSpec((LANES, ROW_TILE), lambda i, j: (0, j))])
    n_q = ROW_TILE // Q_CHUNK
    n_k = ROW_TILE // K_CHUNK
    out_shape = [
        jax.ShapeDtypeStruct((b, HEADS, seq // Q_CHUNK, QK_PAD, Q_CHUNK), BF16),
        jax.ShapeDtypeStruct((b, HEADS, seq, QK_PAD), BF16),
        jax.ShapeDtypeStruct((b, HEADS, seq // K_CHUNK, V_HEAD, K_CHUNK), BF16),
        jax.ShapeDtypeStruct((b, seq, CONV_WIDTH), BF16),
        jax.ShapeDtypeStruct((b, seq, CONV_WIDTH), BF16),
    ]
    out_specs = [
        pl.BlockSpec((1, HEADS, n_q, QK_PAD, Q_CHUNK), lambda i, j: (i, 0, j, 0, 0)),
        pl.BlockSpec((1, HEADS, ROW_TILE, QK_PAD), lambda i, j: (i, 0, j, 0)),
        pl.BlockSpec((1, HEADS, n_k, V_HEAD, K_CHUNK), lambda i, j: (i, 0, j, 0, 0)),
        pl.BlockSpec((1, ROW_TILE, CONV_WIDTH), lambda i, j: (i, j, 0)),
        pl.BlockSpec((1, ROW_TILE, CONV_WIDTH), lambda i, j: (i, j, 0)),
    ]
    return pl.pallas_call(
        _proj_kernel,
        grid=(b, nt),
        in_specs=in_specs,
        out_specs=out_specs,
        out_shape=out_shape,
        compiler_params=pltpu.CompilerParams(
            dimension_semantics=("parallel", "parallel"), vmem_limit_bytes=VMEM_LIMIT),
        name="proj_real",
    )(x, lw["g_pre"], lw["w_in"], lw["g_q"], lw["wq_t"], lw["g_kv"], lw["wk"], lw["wv_t"],
      cos, sin, cos_t, sin_t)


def _proj_meta(xm, lw, tabs):
    cos, sin, cos_t, sin_t = tabs
    r = META_ROWS
    in_specs = ([_const_spec((r, D_MODEL))] + _proj_weight_specs()
                + [_const_spec((r, LANES)), _const_spec((r, LANES)),
                   _const_spec((LANES, r)), _const_spec((LANES, r))])
    out_shape = [
        jax.ShapeDtypeStruct((HEADS, QK_PAD, r), BF16),
        jax.ShapeDtypeStruct((HEADS, r, QK_PAD), BF16),
        jax.ShapeDtypeStruct((HEADS, V_HEAD, r), BF16),
        jax.ShapeDtypeStruct((r, CONV_WIDTH), BF16),
        jax.ShapeDtypeStruct((r, CONV_WIDTH), BF16),
    ]
    out_specs = [
        pl.BlockSpec((HEADS, QK_PAD, r), lambda i: (0, 0, 0)),
        pl.BlockSpec((HEADS, r, QK_PAD), lambda i: (0, 0, 0)),
        pl.BlockSpec((HEADS, V_HEAD, r), lambda i: (0, 0, 0)),
        pl.BlockSpec((r, CONV_WIDTH), lambda i: (0, 0)),
        pl.BlockSpec((r, CONV_WIDTH), lambda i: (0, 0)),
    ]
    return pl.pallas_call(
        _proj_meta_kernel,
        grid=(1,),
        in_specs=in_specs,
        out_specs=out_specs,
        out_shape=out_shape,
        compiler_params=pltpu.CompilerParams(
            dimension_semantics=("arbitrary",), vmem_limit_bytes=VMEM_LIMIT),
        name="proj_meta",
    )(xm, lw["g_pre"], lw["w_in"], lw["g_q"], lw["wq_t"], lw["g_kv"], lw["wk"], lw["wv_t"],
      cos, sin, cos_t, sin_t)


def _attn_kernel(q_ref, k_ref, v_ref, km_ref, vm_ref, o_ref, s_ref,
                 *, n_qc, n_kc, q_chunk, unroll):
    meta_valid = lax.broadcasted_iota(jnp.int32, (META_PAD, q_chunk), 0) < N_META
    n_groups = n_kc // unroll

    def ones_rows(n):
        first = lax.broadcasted_iota(jnp.int32, (BF16_SUBLANES, n), 0) == 0
        return jnp.where(first, 1.0, 0.0).astype(BF16)

    def put_scores(slot, kc, qc):
        k = k_ref[0, 0, pl.ds(pl.multiple_of(kc * K_CHUNK, K_CHUNK), K_CHUNK), :]
        s = _dot(k, q_ref[0, 0, qc])
        s_ref[slot] = s
        return jnp.max(s, axis=0, keepdims=True)

    def with_ones(v_t):
        return jnp.concatenate([v_t, ones_rows(v_t.shape[1])], axis=0)

    def accumulate(s, s_max, v_t, m, acc):
        m_new = jnp.maximum(m, s_max)
        p = jnp.exp2(s - m_new).astype(BF16)
        return m_new, jnp.exp2(m - m_new) * acc + _dot(with_ones(v_t), p)

    def k_group(base, qc, q_next, carry, last):
        m, acc, s_max = carry
        for t in range(unroll):
            final = last and t == unroll - 1
            if not final:
                next_max = put_scores((t + 1) % 2, base + t + 1, qc)
            elif q_next is not None:
                next_max = put_scores(0, 0, q_next)
            else:
                next_max = s_max
            m, acc = accumulate(s_ref[t % 2], s_max, v_ref[0, 0, base + t], m, acc)
            if last and t == 0:
                s_meta = jnp.where(meta_valid, _dot(km_ref[0, 0], q_ref[0, 0, qc]), NEG_BIG)
                m, acc = accumulate(s_meta, jnp.max(s_meta, axis=0, keepdims=True), vm_ref[0, 0], m, acc)
            s_max = next_max
        return m, acc, s_max

    def q_chunk_body(qc, q_next, s_max):
        m0 = jnp.full((1, q_chunk), NEG_BIG, F32)
        acc0 = jnp.zeros((V_HEAD + BF16_SUBLANES, q_chunk), F32)
        carry = (m0, acc0, s_max)
        if n_groups > 1:
            carry = lax.fori_loop(
                0, n_groups - 1, lambda g, c: k_group(g * unroll, qc, q_next, c, False), carry)
        _, acc, s_max = k_group((n_groups - 1) * unroll, qc, q_next, carry, True)
        out = (acc[:V_HEAD] / acc[V_HEAD:V_HEAD + 1]).T
        o_ref[0, pl.ds(pl.multiple_of(qc * q_chunk, q_chunk), q_chunk), :] = out.astype(o_ref.dtype)
        return s_max

    s_max = put_scores(0, 0, 0)
    if n_qc == 1:
        q_chunk_body(0, None, s_max)
    else:
        lax.fori_loop(
            0, n_qc, lambda qc, s_max: q_chunk_body(qc, jnp.minimum(qc + 1, n_qc - 1), s_max), s_max)


def _attention(q_t, k, v_t, k_meta, v_meta_t, *, q_chunks_per_step):
    b, _, n_q_total, _, q_chunk = q_t.shape
    seq_k = k.shape[2]
    n_kc = seq_k // K_CHUNK
    unroll = min(K_UNROLL, n_kc)
    assert unroll % 2 == 0 and n_kc % unroll == 0
    n_qc = q_chunks_per_step
    n_steps = n_q_total // n_qc
    tq = n_qc * q_chunk
    kern = functools.partial(_attn_kernel, n_qc=n_qc, n_kc=n_kc, q_chunk=q_chunk, unroll=unroll)
    return pl.pallas_call(
        kern,
        grid=(b, HEADS, n_steps),
        in_specs=[
            pl.BlockSpec((1, 1, n_qc, QK_PAD, q_chunk), lambda i, h, j: (i, h, j, 0, 0)),
            pl.BlockSpec((1, 1, seq_k, QK_PAD), lambda i, h, j: (i, h, 0, 0)),
            pl.BlockSpec((1, 1, n_kc, V_HEAD, K_CHUNK), lambda i, h, j: (i, h, 0, 0, 0)),
            pl.BlockSpec((1, 1, META_PAD, QK_PAD), lambda i, h, j: (i, h, 0, 0)),
            pl.BlockSpec((1, 1, V_HEAD, META_PAD), lambda i, h, j: (i, h, 0, 0)),
        ],
        out_specs=pl.BlockSpec((1, tq, V_HEAD), lambda i, h, j: (i, j, h)),
        out_shape=jax.ShapeDtypeStruct((b, n_q_total * q_chunk, MLA_WIDTH), BF16),
        scratch_shapes=[pltpu.VMEM((2, K_CHUNK, q_chunk), F32)],
        compiler_params=pltpu.CompilerParams(
            dimension_semantics=("parallel", "parallel", "parallel"), vmem_limit_bytes=VMEM_LIMIT),
        name="attention",
    )(q_t, k, v_t, k_meta, v_meta_t)


def _mix_ffn_body(x, attn, g_b, u_prev, u_cur, u_next, conv_w, w_out_ref, g_post_mix, g_pre_ffn,
                  w_gate_ref, w_up_ref, w_down_ref, g_post_ffn):
    y = u_prev * conv_w[0:1] + u_cur * conv_w[1:2] + u_next * conv_w[2:3]
    conv = (g_b.astype(F32) * y).astype(BF16)
    mix = _dot(attn, w_out_ref[:MLA_WIDTH, :]) + _dot(conv, w_out_ref[MLA_WIDTH:, :])
    x1 = x + _rms(mix, g_post_mix)
    h = _rms(x1, g_pre_ffn).astype(BF16)
    f = None
    for c0, c1 in FF_CHUNKS:
        gate = _dot(h, w_gate_ref[:, c0:c1])
        up = _dot(h, w_up_ref[:, c0:c1])
        act = (gate / (1.0 + jnp.exp(-gate)) * up).astype(BF16)
        part = _dot(act, w_down_ref[c0:c1, :])
        f = part if f is None else f + part
    return x1 + _rms(f, g_post_ffn)


def _mix_ffn_kernel(x_ref, attn_ref, gb_ref, u_ref, u_before_ref, u_after_ref, u_meta_ref,
                    conv_w_ref, w_out_ref, g_post_mix_ref, g_pre_ffn_ref,
                    w_gate_ref, w_up_ref, w_down_ref, g_post_ffn_ref, o_ref):
    j = pl.program_id(1)
    last = pl.num_programs(1) - 1
    u_cur = u_ref[0].astype(F32)
    tail = BF16_SUBLANES - 1
    before = jnp.where(j == 0, u_meta_ref[0, tail:tail + 1, :], u_before_ref[0, tail:tail + 1, :])
    after = jnp.where(j == last, jnp.zeros_like(u_after_ref[0, 0:1, :]), u_after_ref[0, 0:1, :])
    row = lax.broadcasted_iota(jnp.int32, u_cur.shape, 0)
    u_prev = jnp.where(row == 0, before.astype(F32), pltpu.roll(u_cur, 1, 0))
    u_next = jnp.where(row == ROW_TILE - 1, after.astype(F32), pltpu.roll(u_cur, ROW_TILE - 1, 0))
    o_ref[0] = _mix_ffn_body(
        x_ref[0], attn_ref[0], gb_ref[0], u_prev, u_cur, u_next, conv_w_ref[...], w_out_ref,
        g_post_mix_ref[...], g_pre_ffn_ref[...], w_gate_ref, w_up_ref, w_down_ref,
        g_post_ffn_ref[...])


def _mix_ffn_meta_kernel(x_ref, attn_ref, gb_ref, u_prev_ref, u_ref, u_next_ref,
                         conv_w_ref, w_out_ref, g_post_mix_ref, g_pre_ffn_ref,
                         w_gate_ref, w_up_ref, w_down_ref, g_post_ffn_ref, o_ref):
    o_ref[...] = _mix_ffn_body(
        x_ref[...], attn_ref[...], gb_ref[...], u_prev_ref[...].astype(F32),
        u_ref[...].astype(F32), u_next_ref[...].astype(F32), conv_w_ref[...], w_out_ref,
        g_post_mix_ref[...], g_pre_ffn_ref[...], w_gate_ref, w_up_ref, w_down_ref,
        g_post_ffn_ref[...])


def _ffn_weight_specs():
    return [
        _const_spec((CONV_K, CONV_WIDTH)),
        _const_spec((D_MODEL, D_MODEL)),
        _const_spec((1, D_MODEL)),
        _const_spec((1, D_MODEL)),
        _const_spec((D_MODEL, D_FF)),
        _const_spec((D_MODEL, D_FF)),
        _const_spec((D_FF, D_MODEL)),
        _const_spec((1, D_MODEL)),
    ]


def _ffn_weights(lw):
    return (lw["conv_w"], lw["w_out"], lw["g_post_mix"], lw["g_pre_ffn"], lw["w_gate"], lw["w_up"],
            lw["w_down"], lw["g_post_ffn"])


def _mix_ffn_real(x, attn, g_b, u, u_meta, lw):
    b, seq, _ = x.shape
    nt = seq // ROW_TILE
    halo_per_tile = ROW_TILE // BF16_SUBLANES
    n_halo = seq // BF16_SUBLANES
    tile = lambda w: pl.BlockSpec((1, ROW_TILE, w), lambda i, j: (i, j, 0))
    in_specs = [
        tile(D_MODEL), tile(MLA_WIDTH), tile(CONV_WIDTH), tile(CONV_WIDTH),
        pl.BlockSpec((1, BF16_SUBLANES, CONV_WIDTH),
                     lambda i, j: (i, jnp.maximum(j * halo_per_tile - 1, 0), 0)),
        pl.BlockSpec((1, BF16_SUBLANES, CONV_WIDTH),
                     lambda i, j: (i, jnp.minimum((j + 1) * halo_per_tile, n_halo - 1), 0)),
        pl.BlockSpec((1, BF16_SUBLANES, CONV_WIDTH), lambda i, j: (i, 0, 0)),
    ] + _ffn_weight_specs()
    return pl.pallas_call(
        _mix_ffn_kernel,
        grid=(b, nt),
        in_specs=in_specs,
        out_specs=tile(D_MODEL),
        out_shape=jax.ShapeDtypeStruct((b, seq, D_MODEL), F32),
        compiler_params=pltpu.CompilerParams(
            dimension_semantics=("parallel", "parallel"), vmem_limit_bytes=VMEM_LIMIT),
        name="mix_ffn_real",
    )(x, attn, g_b, u, u, u, u_meta, *_ffn_weights(lw))


def _mix_ffn_meta(xm, attn, g_b, u_prev, u, u_next, lw):
    r = META_ROWS
    in_specs = [_const_spec((r, D_MODEL)), _const_spec((r, MLA_WIDTH))] \
        + [_const_spec((r, CONV_WIDTH)) for _ in range(4)] + _ffn_weight_specs()
    return pl.pallas_call(
        _mix_ffn_meta_kernel,
        grid=(1,),
        in_specs=in_specs,
        out_specs=pl.BlockSpec((r, D_MODEL), lambda i: (0, 0)),
        out_shape=jax.ShapeDtypeStruct((r, D_MODEL), F32),
        compiler_params=pltpu.CompilerParams(
            dimension_semantics=("arbitrary",), vmem_limit_bytes=VMEM_LIMIT),
        name="mix_ffn_meta",
    )(xm, attn, g_b, u_prev, u, u_next, *_ffn_weights(lw))


def _pack_layer(i, pre_mix_g, w_in, q_norm_g, w_q_up, kv_norm_g, w_kv_up, conv_w, w_out,
                post_mix_g, pre_ffn_g, w_gate, w_up, w_down, post_ffn_g):
    wi = w_in[i]
    i1 = Q_LORA + KV_LORA
    i2 = i1 + QK_ROPE
    k_r = wi[:, i1:i2]
    zpad = jnp.zeros((D_MODEL, LANES - QK_ROPE), F32)
    k_a = jnp.concatenate([k_r, zpad], axis=1)
    k_b = jnp.concatenate([k_r[:, ROPE_HALF:], k_r[:, :ROPE_HALF], zpad], axis=1)
    w_in_p = jnp.concatenate([wi[:, :i1], k_a, k_b, wi[:, i2:]], axis=1).astype(BF16)

    wq = w_q_up[i].reshape(Q_LORA, HEADS, QK_NOPE + QK_ROPE)
    qz = jnp.zeros((Q_LORA, HEADS, LANES - QK_ROPE), F32)
    q_rope = wq[:, :, QK_NOPE:]
    wq_p = jnp.concatenate(
        [wq[:, :, :QK_NOPE], q_rope, qz,
         q_rope[:, :, ROPE_HALF:], q_rope[:, :, :ROPE_HALF], qz], axis=2)
    wq_t = wq_p.reshape(Q_LORA, Q_ROWS_P).T.astype(BF16)

    wkv = w_kv_up[i].reshape(KV_LORA, HEADS, QK_NOPE + V_HEAD)
    wk = wkv[:, :, :QK_NOPE].reshape(KV_LORA, HEADS * QK_NOPE).astype(BF16)
    wv_t = wkv[:, :, QK_NOPE:].reshape(KV_LORA, HEADS * V_HEAD).T.astype(BF16)

    row = lambda g: g[i].reshape(1, -1).astype(F32)
    return dict(
        g_pre=row(pre_mix_g), w_in=w_in_p, g_q=row(q_norm_g), wq_t=wq_t, g_kv=row(kv_norm_g),
        wk=wk, wv_t=wv_t, conv_w=conv_w[i].astype(F32), w_out=w_out[i].astype(BF16),
        g_post_mix=row(post_mix_g), g_pre_ffn=row(pre_ffn_g), w_gate=w_gate[i].astype(BF16),
        w_up=w_up[i].astype(BF16), w_down=w_down[i].astype(BF16), g_post_ffn=row(post_ffn_g))


def _rope_tables(pos):
    inv_freq = ROPE_THETA ** (-jnp.arange(0, QK_ROPE, 2, dtype=F32) / QK_ROPE)
    ang = pos.astype(F32)[:, None] * inv_freq[None, :]
    c, s = jnp.cos(ang), jnp.sin(ang)
    z = jnp.zeros((pos.shape[0], LANES - QK_ROPE), F32)
    cos = jnp.concatenate([c, c, z], axis=1)
    sin = jnp.concatenate([-s, s, z], axis=1)
    return cos, sin, cos.T, sin.T


def _meta_to_groups(a, batches, axis):
    out, r0 = [], 0
    for b in batches:
        sl = lax.slice_in_dim(a, r0, r0 + b * N_META, axis=axis)
        shape = a.shape[:axis] + (b, N_META) + a.shape[axis + 1:]
        out.append(sl.reshape(shape))
        r0 += b * N_META
    return out


def _join_meta(parts):
    flat = [p.reshape(-1, p.shape[-1]) for p in parts]
    rows = sum(f.shape[0] for f in flat)
    flat.append(jnp.zeros((META_ROWS - rows, flat[0].shape[-1]), flat[0].dtype))
    return jnp.concatenate(flat, axis=0)


def kernel(x_prompt, x_sample, meta_tokens, pre_mix_g, w_in, q_norm_g, w_q_up, kv_norm_g, w_kv_up,
           conv_w, w_out, post_mix_g, pre_ffn_g, w_gate, w_up, w_down, post_ffn_g):
    xs = [x_prompt, x_sample]
    batches = [x.shape[0] for x in xs]
    assert sum(batches) * N_META <= META_ROWS
    for x in xs:
        assert x.shape[1] % ROW_TILE == 0 and x.shape[2] == D_MODEL

    real_tabs = [_rope_tables(N_META + jnp.arange(x.shape[1])) for x in xs]
    meta_pos = jnp.arange(META_ROWS) % N_META
    meta_tabs = _rope_tables(meta_pos)

    xm = _join_meta([jnp.broadcast_to(meta_tokens.astype(F32)[None], (b, N_META, D_MODEL))
                     for b in batches])

    for layer in range(DEPTH):
        lw = _pack_layer(layer, pre_mix_g, w_in, q_norm_g, w_q_up, kv_norm_g, w_kv_up, conv_w,
                         w_out, post_mix_g, pre_ffn_g, w_gate, w_up, w_down, post_ffn_g)
        qm, km, vm, gbm, um = _proj_meta(xm, lw, meta_tabs)
        qm_g = _meta_to_groups(qm, batches, 2)
        km_g = _meta_to_groups(km, batches, 1)
        vm_g = _meta_to_groups(vm, batches, 2)
        um_g = _meta_to_groups(um, batches, 0)

        new_xs, attn_meta, u_prev_meta, u_next_meta = [], [], [], []
        for g, x in enumerate(xs):
            q_t, k, v_t, g_b, u = _proj_real(x, lw, real_tabs[g])
            k_meta = jnp.pad(jnp.transpose(km_g[g], (1, 0, 2, 3)),
                             ((0, 0), (0, 0), (0, META_PAD - N_META), (0, 0)))
            v_meta_t = jnp.pad(jnp.transpose(vm_g[g], (2, 0, 1, 3)),
                               ((0, 0), (0, 0), (0, 0), (0, META_PAD - N_META)))
            q_meta_t = jnp.pad(jnp.transpose(qm_g[g], (2, 0, 1, 3)),
                               ((0, 0), (0, 0), (0, 0), (0, META_PAD - N_META)))[:, :, None]
            n_q_total = x.shape[1] // Q_CHUNK
            attn = _attention(q_t, k, v_t, k_meta, v_meta_t,
                              q_chunks_per_step=min(n_q_total, 4))
            attn_m = _attention(q_meta_t, k, v_t, k_meta, v_meta_t, q_chunks_per_step=1)
            attn_meta.append(attn_m[:, :N_META])
            um_b = um_g[g]
            zero_row = jnp.zeros_like(um_b[:, :1])
            u_prev_meta.append(jnp.concatenate([zero_row, um_b[:, :-1]], axis=1))
            u_next_meta.append(jnp.concatenate([um_b[:, 1:], u[:, :1]], axis=1))
            new_xs.append(_mix_ffn_real(x, attn, g_b, u, um_b, lw))

        xm = _mix_ffn_meta(xm, _join_meta(attn_meta), gbm, _join_meta(u_prev_meta), um,
                           _join_meta(u_next_meta), lw)
        xs = new_xs

    return (xs[0], xs[1])
```

```python
import functools

import jax
import jax.numpy as jnp
from jax import lax
from jax.experimental import pallas as pl
from jax.experimental.pallas import tpu as pltpu

F32 = jnp.float32
BF16 = jnp.bfloat16

D_MODEL = 1024
DEPTH = 2
N_META = 16
HEADS = 4
QK_NOPE = 128
QK_ROPE = 64
ROPE_HALF = QK_ROPE // 2
V_HEAD = 128
MLA_WIDTH = HEADS * V_HEAD
CONV_WIDTH = D_MODEL - MLA_WIDTH
CONV_K = 3
Q_LORA = 384
KV_LORA = 256
D_FF = 2816
ROPE_THETA = 10000.0
EPS = 1e-6
ATTN_SCALE = (QK_NOPE + QK_ROPE) ** -0.5
LOG2_E = 1.4426950408889634
Q_SCALE = ATTN_SCALE * LOG2_E

LANES = 128
BF16_SUBLANES = 16
MXU_DIM = 256
VMEM_LIMIT = 56 * 1024 * 1024

QK_PAD = QK_NOPE + LANES
META_PAD = LANES
ROW_TILE = 512
Q_PER_STEP = 2048
K_CHUNK = 512
K_UNROLL = 8
META_ROWS = 384
FF_CHUNKS = ((0, 1280), (1280, 2816))

C_Q0, C_KV0, C_KR0 = 0, Q_LORA, Q_LORA + KV_LORA
C_GB0 = C_KR0 + LANES
C_GC0 = C_GB0 + CONV_WIDTH
C_CH0 = C_GC0 + CONV_WIDTH
IN_COLS_P = C_CH0 + CONV_WIDTH
Q_HEAD_ROWS = QK_NOPE + QK_ROPE

NEG_BIG = -1e30


def _rms(x, g):
    ms = jnp.mean(x * x, axis=-1, keepdims=True)
    return x * lax.rsqrt(ms + EPS) * g


def _dot(a, b):
    return jnp.dot(a, b, preferred_element_type=F32)


def _dot_nt(a, b):
    return lax.dot_general(a, b, (((1,), (1,)), ((), ())), preferred_element_type=F32)


def _proj_body(x, g_pre, w_in_ref, g_q, wq_t_ref, g_kv, wk_ref, wv_t_ref, cos, sin, cos_t, sin_t):
    h = _rms(x, g_pre).astype(BF16)
    z = _dot(h, w_in_ref[...])
    cqn = _rms(z[:, C_Q0:C_KV0], g_q).astype(BF16)
    ckvn = _rms(z[:, C_KV0:C_KR0], g_kv).astype(BF16)
    k_r = z[:, C_KR0:C_GB0]
    lane = lax.broadcasted_iota(jnp.int32, k_r.shape, 1)
    partner = jnp.where(lane < ROPE_HALF, pltpu.roll(k_r, LANES - ROPE_HALF, 1),
                        pltpu.roll(k_r, ROPE_HALF, 1))
    k_rot = k_r * cos + partner * sin
    g_b = z[:, C_GB0:C_GC0]
    u = z[:, C_GC0:C_CH0] * z[:, C_CH0:IN_COLS_P]
    q_t = _dot_nt(wq_t_ref[...], cqn)
    k_nope = _dot(ckvn, wk_ref[...])
    v_t = _dot_nt(wv_t_ref[...], ckvn)
    q_heads = []
    for hd in range(HEADS):
        r0 = hd * Q_HEAD_ROWS
        nope = q_t[r0:r0 + QK_NOPE]
        x1 = q_t[r0 + QK_NOPE:r0 + QK_NOPE + ROPE_HALF]
        x2 = q_t[r0 + QK_NOPE + ROPE_HALF:r0 + Q_HEAD_ROWS]
        rot = jnp.concatenate([x1 * cos_t - x2 * sin_t, x2 * cos_t + x1 * sin_t,
                               jnp.zeros((QK_PAD - Q_HEAD_ROWS, q_t.shape[1]), F32)], axis=0)
        q_heads.append(((nope * Q_SCALE).astype(BF16), (rot * Q_SCALE).astype(BF16)))
    return q_heads, k_nope.astype(BF16), k_rot.astype(BF16), v_t.astype(BF16), g_b.astype(BF16), u.astype(BF16)


def _proj_kernel(x_ref, g_pre_ref, w_in_ref, g_q_ref, wq_t_ref, g_kv_ref, wk_ref, wv_t_ref,
                 cos_ref, sin_ref, cos_t_ref, sin_t_ref,
                 q_ref, k_ref, v_ref, gb_ref, u_ref):
    q_heads, k_nope, k_rot, v_t, g_b, u = _proj_body(
        x_ref[0], g_pre_ref[...], w_in_ref, g_q_ref[...], wq_t_ref, g_kv_ref[...], wk_ref,
        wv_t_ref, cos_ref[...], sin_ref[...], cos_t_ref[...], sin_t_ref[...])
    n_k = ROW_TILE // K_CHUNK
    for hd in range(HEADS):
        nope, rot = q_heads[hd]
        q_ref[0, hd, 0, :QK_NOPE, :] = nope
        q_ref[0, hd, 0, QK_NOPE:, :] = rot
        k_ref[0, hd, :, :QK_NOPE] = k_nope[:, hd * QK_NOPE:(hd + 1) * QK_NOPE]
        k_ref[0, hd, :, QK_NOPE:] = k_rot
        for c in range(n_k):
            v_ref[0, hd, c] = v_t[hd * V_HEAD:(hd + 1) * V_HEAD, c * K_CHUNK:(c + 1) * K_CHUNK]
    gb_ref[0] = g_b
    u_ref[0] = u


def _proj_meta_kernel(x_ref, g_pre_ref, w_in_ref, g_q_ref, wq_t_ref, g_kv_ref, wk_ref, wv_t_ref,
                      cos_ref, sin_ref, cos_t_ref, sin_t_ref,
                      q_ref, k_ref, v_ref, gb_ref, u_ref):
    q_heads, k_nope, k_rot, v_t, g_b, u = _proj_body(
        x_ref[...], g_pre_ref[...], w_in_ref, g_q_ref[...], wq_t_ref, g_kv_ref[...], wk_ref,
        wv_t_ref, cos_ref[...], sin_ref[...], cos_t_ref[...], sin_t_ref[...])
    for hd in range(HEADS):
        nope, rot = q_heads[hd]
        q_ref[hd, :QK_NOPE, :] = nope
        q_ref[hd, QK_NOPE:, :] = rot
        k_ref[hd, :, :QK_NOPE] = k_nope[:, hd * QK_NOPE:(hd + 1) * QK_NOPE]
        k_ref[hd, :, QK_NOPE:] = k_rot
        v_ref[hd] = v_t[hd * V_HEAD:(hd + 1) * V_HEAD, :]
    gb_ref[...] = g_b
    u_ref[...] = u


def _const_spec(shape):
    zeros = (0,) * len(shape)
    return pl.BlockSpec(shape, lambda *_: zeros, pipeline_mode=pl.Buffered(1))


def _proj_weight_specs():
    return [
        _const_spec((1, D_MODEL)),
        _const_spec((D_MODEL, IN_COLS_P)),
        _const_spec((1, Q_LORA)),
        _const_spec((HEADS * Q_HEAD_ROWS, Q_LORA)),
        _const_spec((1, KV_LORA)),
        _const_spec((KV_LORA, HEADS * QK_NOPE)),
        _const_spec((HEADS * V_HEAD, KV_LORA)),
    ]


def _proj_real(x, lw, tabs, q_chunk):
    b, seq, _ = x.shape
    nt = seq // ROW_TILE
    cos, sin, cos_t, sin_t = tabs
    in_specs = ([pl.BlockSpec((1, ROW_TILE, D_MODEL), lambda i, j: (i, j, 0))]
                + _proj_weight_specs()
                + [pl.BlockSpec((ROW_TILE, LANES), lambda i, j: (j, 0)),
                   pl.BlockSpec((ROW_TILE, LANES), lambda i, j: (j, 0)),
                   pl.BlockSpec((ROPE_HALF, ROW_TILE), lambda i, j: (0, j)),
                   pl.BlockSpec((ROPE_HALF, ROW_TILE), lambda i, j: (0, j))])
    tiles_per_q = q_chunk // ROW_TILE
    n_k = ROW_TILE // K_CHUNK
    out_shape = [
        jax.ShapeDtypeStruct((b, HEADS, seq // q_chunk, QK_PAD, q_chunk), BF16),
        jax.ShapeDtypeStruct((b, HEADS, seq, QK_PAD), BF16),
        jax.ShapeDtypeStruct((b, HEADS, seq // K_CHUNK, V_HEAD, K_CHUNK), BF16),
        jax.ShapeDtypeStruct((b, seq, CONV_WIDTH), BF16),
        jax.ShapeDtypeStruct((b, seq, CONV_WIDTH), BF16),
    ]
    out_specs = [
        pl.BlockSpec((1, HEADS, 1, QK_PAD, ROW_TILE),
                     lambda i, j: (i, 0, j // tiles_per_q, 0, j % tiles_per_q)),
        pl.BlockSpec((1, HEADS, ROW_TILE, QK_PAD), lambda i, j: (i, 0, j, 0)),
        pl.BlockSpec((1, HEADS, n_k, V_HEAD, K_CHUNK), lambda i, j: (i, 0, j, 0, 0)),
        pl.BlockSpec((1, ROW_TILE, CONV_WIDTH), lambda i, j: (i, j, 0)),
        pl.BlockSpec((1, ROW_TILE, CONV_WIDTH), lambda i, j: (i, j, 0)),
    ]
    return pl.pallas_call(
        _proj_kernel,
        grid=(b, nt),
        in_specs=in_specs,
        out_specs=out_specs,
        out_shape=out_shape,
        compiler_params=pltpu.CompilerParams(
            dimension_semantics=("parallel", "parallel"), vmem_limit_bytes=VMEM_LIMIT),
        name="proj_real",
    )(x, lw["g_pre"], lw["w_in"], lw["g_q"], lw["wq_t"], lw["g_kv"], lw["wk"], lw["wv_t"],
      cos, sin, cos_t, sin_t)


def _proj_meta(xm, lw, tabs):
    cos, sin, cos_t, sin_t = tabs
    r = META_ROWS
    in_specs = ([_const_spec((r, D_MODEL))] + _proj_weight_specs()
                + [_const_spec((r, LANES)), _const_spec((r, LANES)),
                   _const_spec((ROPE_HALF, r)), _const_spec((ROPE_HALF, r))])
    out_shape = [
        jax.ShapeDtypeStruct((HEADS, QK_PAD, r), BF16),
        jax.ShapeDtypeStruct((HEADS, r, QK_PAD), BF16),
        jax.ShapeDtypeStruct((HEADS, V_HEAD, r), BF16),
        jax.ShapeDtypeStruct((r, CONV_WIDTH), BF16),
        jax.ShapeDtypeStruct((r, CONV_WIDTH), BF16),
    ]
    out_specs = [
        pl.BlockSpec((HEADS, QK_PAD, r), lambda i: (0, 0, 0)),
        pl.BlockSpec((HEADS, r, QK_PAD), lambda i: (0, 0, 0)),
        pl.BlockSpec((HEADS, V_HEAD, r), lambda i: (0, 0, 0)),
        pl.BlockSpec((r, CONV_WIDTH), lambda i: (0, 0)),
        pl.BlockSpec((r, CONV_WIDTH), lambda i: (0, 0)),
    ]
    return pl.pallas_call(
        _proj_meta_kernel,
        grid=(1,),
        in_specs=in_specs,
        out_specs=out_specs,
        out_shape=out_shape,
        compiler_params=pltpu.CompilerParams(
            dimension_semantics=("arbitrary",), vmem_limit_bytes=VMEM_LIMIT),
        name="proj_meta",
    )(xm, lw["g_pre"], lw["w_in"], lw["g_q"], lw["wq_t"], lw["g_kv"], lw["wk"], lw["wv_t"],
      cos, sin, cos_t, sin_t)


def _attn_kernel(q_ref, k_ref, v_ref, km_ref, vm_ref, o_ref, s_ref,
                 *, n_qc, n_kc, q_chunk, unroll):
    meta_valid = lax.broadcasted_iota(jnp.int32, (META_PAD, q_chunk), 0) < N_META
    n_groups = n_kc // unroll

    def ones_rows(n):
        first = lax.broadcasted_iota(jnp.int32, (BF16_SUBLANES, n), 0) == 0
        return jnp.where(first, 1.0, 0.0).astype(BF16)

    def put_scores(slot, kc, qc):
        k = k_ref[0, 0, pl.ds(pl.multiple_of(kc * K_CHUNK, K_CHUNK), K_CHUNK), :]
        s = _dot(k, q_ref[0, 0, qc])
        s_ref[slot] = s
        return jnp.max(s, axis=0, keepdims=True)

    def with_ones(v_t):
        return jnp.concatenate([v_t, ones_rows(v_t.shape[1])], axis=0)

    def accumulate(s, s_max, v_t, m, acc):
        m_new = jnp.maximum(m, s_max)
        p = jnp.exp2(s - m_new).astype(BF16)
        return m_new, jnp.exp2(m - m_new) * acc + _dot(with_ones(v_t), p)

    def k_group(base, qc, q_next, carry, last):
        m, acc, s_max = carry
        for t in range(unroll):
            final = last and t == unroll - 1
            if not final:
                next_max = put_scores((t + 1) % 2, base + t + 1, qc)
            elif q_next is not None:
                next_max = put_scores(0, 0, q_next)
            else:
                next_max = s_max
            m, acc = accumulate(s_ref[t % 2], s_max, v_ref[0, 0, base + t], m, acc)
            if last and t == 0:
                s_meta = jnp.where(meta_valid, _dot(km_ref[0, 0], q_ref[0, 0, qc]), NEG_BIG)
                m, acc = accumulate(s_meta, jnp.max(s_meta, axis=0, keepdims=True), vm_ref[0, 0], m, acc)
            s_max = next_max
        return m, acc, s_max

    def q_chunk_body(qc, q_next, s_max):
        m0 = jnp.full((1, q_chunk), NEG_BIG, F32)
        acc0 = jnp.zeros((V_HEAD + BF16_SUBLANES, q_chunk), F32)
        carry = (m0, acc0, s_max)
        if n_groups > 1:
            carry = lax.fori_loop(
                0, n_groups - 1, lambda g, c: k_group(g * unroll, qc, q_next, c, False), carry)
        _, acc, s_max = k_group((n_groups - 1) * unroll, qc, q_next, carry, True)
        out = (acc[:V_HEAD] / acc[V_HEAD:V_HEAD + 1]).T
        o_ref[0, pl.ds(pl.multiple_of(qc * q_chunk, q_chunk), q_chunk), :] = out.astype(o_ref.dtype)
        return s_max

    s_max = put_scores(0, 0, 0)
    if n_qc == 1:
        q_chunk_body(0, None, s_max)
    else:
        lax.fori_loop(
            0, n_qc, lambda qc, s_max: q_chunk_body(qc, jnp.minimum(qc + 1, n_qc - 1), s_max), s_max)


def _attention(q_t, k, v_t, k_meta, v_meta_t, *, q_chunks_per_step):
    b, _, n_q_total, _, q_chunk = q_t.shape
    seq_k = k.shape[2]
    n_kc = seq_k // K_CHUNK
    unroll = min(K_UNROLL, n_kc)
    assert unroll % 2 == 0 and n_kc % unroll == 0
    n_qc = q_chunks_per_step
    n_steps = n_q_total // n_qc
    tq = n_qc * q_chunk
    kern = functools.partial(_attn_kernel, n_qc=n_qc, n_kc=n_kc, q_chunk=q_chunk, unroll=unroll)
    return pl.pallas_call(
        kern,
        grid=(b, HEADS, n_steps),
        in_specs=[
            pl.BlockSpec((1, 1, n_qc, QK_PAD, q_chunk), lambda i, h, j: (i, h, j, 0, 0)),
            pl.BlockSpec((1, 1, seq_k, QK_PAD), lambda i, h, j: (i, h, 0, 0)),
            pl.BlockSpec((1, 1, n_kc, V_HEAD, K_CHUNK), lambda i, h, j: (i, h, 0, 0, 0)),
            pl.BlockSpec((1, 1, META_PAD, QK_PAD), lambda i, h, j: (i, h, 0, 0)),
            pl.BlockSpec((1, 1, V_HEAD, META_PAD), lambda i, h, j: (i, h, 0, 0)),
        ],
        out_specs=pl.BlockSpec((1, tq, V_HEAD), lambda i, h, j: (i, j, h)),
        out_shape=jax.ShapeDtypeStruct((b, n_q_total * q_chunk, MLA_WIDTH), BF16),
        scratch_shapes=[pltpu.VMEM((2, K_CHUNK, q_chunk), F32)],
        compiler_params=pltpu.CompilerParams(
            dimension_semantics=("parallel", "parallel", "parallel"), vmem_limit_bytes=VMEM_LIMIT),
        name="attention",
    )(q_t, k, v_t, k_meta, v_meta_t)


def _mix_ffn_body(x, attn, g_b, u_prev, u_cur, u_next, conv_w, w_out_ref, g_post_mix, g_pre_ffn,
                  w_gate_ref, w_up_ref, w_down_ref, g_post_ffn):
    y = u_prev * conv_w[0:1] + u_cur * conv_w[1:2] + u_next * conv_w[2:3]
    conv = (g_b.astype(F32) * y).astype(BF16)
    mix = _dot(attn, w_out_ref[:MLA_WIDTH, :]) + _dot(conv, w_out_ref[MLA_WIDTH:, :])
    x1 = x + _rms(mix, g_post_mix)
    h = _rms(x1, g_pre_ffn).astype(BF16)
    f = None
    for c0, c1 in FF_CHUNKS:
        gate = _dot(h, w_gate_ref[:, c0:c1])
        up = _dot(h, w_up_ref[:, c0:c1])
        act = (gate / (1.0 + jnp.exp(-gate)) * up).astype(BF16)
        part = _dot(act, w_down_ref[c0:c1, :])
        f = part if f is None else f + part
    return x1 + _rms(f, g_post_ffn)


def _mix_ffn_kernel(x_ref, attn_ref, gb_ref, u_ref, u_before_ref, u_after_ref, u_meta_ref,
                    conv_w_ref, w_out_ref, g_post_mix_ref, g_pre_ffn_ref,
                    w_gate_ref, w_up_ref, w_down_ref, g_post_ffn_ref, o_ref):
    j = pl.program_id(1)
    last = pl.num_programs(1) - 1
    u_cur = u_ref[0].astype(F32)
    tail = BF16_SUBLANES - 1
    before = jnp.where(j == 0, u_meta_ref[0, tail:tail + 1, :], u_before_ref[0, tail:tail + 1, :])
    after = jnp.where(j == last, jnp.zeros_like(u_after_ref[0, 0:1, :]), u_after_ref[0, 0:1, :])
    row = lax.broadcasted_iota(jnp.int32, u_cur.shape, 0)
    u_prev = jnp.where(row == 0, before.astype(F32), pltpu.roll(u_cur, 1, 0))
    u_next = jnp.where(row == ROW_TILE - 1, after.astype(F32), pltpu.roll(u_cur, ROW_TILE - 1, 0))
    o_ref[0] = _mix_ffn_body(
        x_ref[0], attn_ref[0], gb_ref[0], u_prev, u_cur, u_next, conv_w_ref[...], w_out_ref,
        g_post_mix_ref[...], g_pre_ffn_ref[...], w_gate_ref, w_up_ref, w_down_ref,
        g_post_ffn_ref[...])


def _mix_ffn_meta_kernel(x_ref, attn_ref, gb_ref, u_prev_ref, u_ref, u_next_ref,
                         conv_w_ref, w_out_ref, g_post_mix_ref, g_pre_ffn_ref,
                         w_gate_ref, w_up_ref, w_down_ref, g_post_ffn_ref, o_ref):
    o_ref[...] = _mix_ffn_body(
        x_ref[...], attn_ref[...], gb_ref[...], u_prev_ref[...].astype(F32),
        u_ref[...].astype(F32), u_next_ref[...].astype(F32), conv_w_ref[...], w_out_ref,
        g_post_mix_ref[...], g_pre_ffn_ref[...], w_gate_ref, w_up_ref, w_down_ref,
        g_post_ffn_ref[...])


def _ffn_weight_specs():
    return [
        _const_spec((CONV_K, CONV_WIDTH)),
        _const_spec((D_MODEL, D_MODEL)),
        _const_spec((1, D_MODEL)),
        _const_spec((1, D_MODEL)),
        _const_spec((D_MODEL, D_FF)),
        _const_spec((D_MODEL, D_FF)),
        _const_spec((D_FF, D_MODEL)),
        _const_spec((1, D_MODEL)),
    ]


def _ffn_weights(lw):
    return (lw["conv_w"], lw["w_out"], lw["g_post_mix"], lw["g_pre_ffn"], lw["w_gate"], lw["w_up"],
            lw["w_down"], lw["g_post_ffn"])


def _mix_ffn_real(x, attn, g_b, u, u_meta, lw):
    b, seq, _ = x.shape
    nt = seq // ROW_TILE
    halo_per_tile = ROW_TILE // BF16_SUBLANES
    n_halo = seq // BF16_SUBLANES
    tile = lambda w: pl.BlockSpec((1, ROW_TILE, w), lambda i, j: (i, j, 0))
    in_specs = [
        tile(D_MODEL), tile(MLA_WIDTH), tile(CONV_WIDTH), tile(CONV_WIDTH),
        pl.BlockSpec((1, BF16_SUBLANES, CONV_WIDTH),
                     lambda i, j: (i, jnp.maximum(j * halo_per_tile - 1, 0), 0)),
        pl.BlockSpec((1, BF16_SUBLANES, CONV_WIDTH),
                     lambda i, j: (i, jnp.minimum((j + 1) * halo_per_tile, n_halo - 1), 0)),
        pl.BlockSpec((1, BF16_SUBLANES, CONV_WIDTH), lambda i, j: (i, 0, 0)),
    ] + _ffn_weight_specs()
    return pl.pallas_call(
        _mix_ffn_kernel,
        grid=(b, nt),
        in_specs=in_specs,
        out_specs=tile(D_MODEL),
        out_shape=jax.ShapeDtypeStruct((b, seq, D_MODEL), F32),
        compiler_params=pltpu.CompilerParams(
            dimension_semantics=("parallel", "parallel"), vmem_limit_bytes=VMEM_LIMIT),
        name="mix_ffn_real",
    )(x, attn, g_b, u, u, u, u_meta, *_ffn_weights(lw))


def _mix_ffn_meta(xm, attn, g_b, u_prev, u, u_next, lw):
    r = META_ROWS
    in_specs = [_const_spec((r, D_MODEL)), _const_spec((r, MLA_WIDTH))] \
        + [_const_spec((r, CONV_WIDTH)) for _ in range(4)] + _ffn_weight_specs()
    return pl.pallas_call(
        _mix_ffn_meta_kernel,
        grid=(1,),
        in_specs=in_specs,
        out_specs=pl.BlockSpec((r, D_MODEL), lambda i: (0, 0)),
        out_shape=jax.ShapeDtypeStruct((r, D_MODEL), F32),
        compiler_params=pltpu.CompilerParams(
            dimension_semantics=("arbitrary",), vmem_limit_bytes=VMEM_LIMIT),
        name="mix_ffn_meta",
    )(xm, attn, g_b, u_prev, u, u_next, *_ffn_weights(lw))


def _pack_layer(i, pre_mix_g, w_in, q_norm_g, w_q_up, kv_norm_g, w_kv_up, conv_w, w_out,
                post_mix_g, pre_ffn_g, w_gate, w_up, w_down, post_ffn_g):
    wi = w_in[i]
    i1 = Q_LORA + KV_LORA
    i2 = i1 + QK_ROPE
    zpad = jnp.zeros((D_MODEL, LANES - QK_ROPE), F32)
    w_in_p = jnp.concatenate([wi[:, :i2], zpad, wi[:, i2:]], axis=1).astype(BF16)
    wq_t = w_q_up[i].T.astype(BF16)

    wkv = w_kv_up[i].reshape(KV_LORA, HEADS, QK_NOPE + V_HEAD)
    wk = wkv[:, :, :QK_NOPE].reshape(KV_LORA, HEADS * QK_NOPE).astype(BF16)
    wv_t = wkv[:, :, QK_NOPE:].reshape(KV_LORA, HEADS * V_HEAD).T.astype(BF16)

    row = lambda g: g[i].reshape(1, -1).astype(F32)
    return dict(
        g_pre=row(pre_mix_g), w_in=w_in_p, g_q=row(q_norm_g), wq_t=wq_t, g_kv=row(kv_norm_g),
        wk=wk, wv_t=wv_t, conv_w=conv_w[i].astype(F32), w_out=w_out[i].astype(BF16),
        g_post_mix=row(post_mix_g), g_pre_ffn=row(pre_ffn_g), w_gate=w_gate[i].astype(BF16),
        w_up=w_up[i].astype(BF16), w_down=w_down[i].astype(BF16), g_post_ffn=row(post_ffn_g))


def _rope_tables(pos):
    inv_freq = ROPE_THETA ** (-jnp.arange(0, QK_ROPE, 2, dtype=F32) / QK_ROPE)
    ang = pos.astype(F32)[:, None] * inv_freq[None, :]
    c, s = jnp.cos(ang), jnp.sin(ang)
    z = jnp.zeros((pos.shape[0], LANES - QK_ROPE), F32)
    cos = jnp.concatenate([c, c, z], axis=1)
    sin = jnp.concatenate([-s, s, z], axis=1)
    return cos, sin, c.T, s.T


def _query_chunk(seq):
    return 1024 if seq // K_CHUNK <= K_UNROLL else 512


def _meta_to_groups(a, batches, axis):
    out, r0 = [], 0
    for b in batches:
        sl = lax.slice_in_dim(a, r0, r0 + b * N_META, axis=axis)
        shape = a.shape[:axis] + (b, N_META) + a.shape[axis + 1:]
        out.append(sl.reshape(shape))
        r0 += b * N_META
    return out


def _join_meta(parts):
    flat = [p.reshape(-1, p.shape[-1]) for p in parts]
    rows = sum(f.shape[0] for f in flat)
    flat.append(jnp.zeros((META_ROWS - rows, flat[0].shape[-1]), flat[0].dtype))
    return jnp.concatenate(flat, axis=0)


def kernel(x_prompt, x_sample, meta_tokens, pre_mix_g, w_in, q_norm_g, w_q_up, kv_norm_g, w_kv_up,
           conv_w, w_out, post_mix_g, pre_ffn_g, w_gate, w_up, w_down, post_ffn_g):
    xs = [x_prompt, x_sample]
    batches = [x.shape[0] for x in xs]
    assert sum(batches) * N_META <= META_ROWS
    for x in xs:
        assert x.shape[1] % ROW_TILE == 0 and x.shape[2] == D_MODEL

    real_tabs = [_rope_tables(N_META + jnp.arange(x.shape[1])) for x in xs]
    meta_pos = jnp.arange(META_ROWS) % N_META
    meta_tabs = _rope_tables(meta_pos)

    xm = _join_meta([jnp.broadcast_to(meta_tokens.astype(F32)[None], (b, N_META, D_MODEL))
                     for b in batches])

    for layer in range(DEPTH):
        lw = _pack_layer(layer, pre_mix_g, w_in, q_norm_g, w_q_up, kv_norm_g, w_kv_up, conv_w,
                         w_out, post_mix_g, pre_ffn_g, w_gate, w_up, w_down, post_ffn_g)
        qm, km, vm, gbm, um = _proj_meta(xm, lw, meta_tabs)
        qm_g = _meta_to_groups(qm, batches, 2)
        km_g = _meta_to_groups(km, batches, 1)
        vm_g = _meta_to_groups(vm, batches, 2)
        um_g = _meta_to_groups(um, batches, 0)

        new_xs, attn_meta, u_prev_meta, u_next_meta = [], [], [], []
        for g, x in enumerate(xs):
            q_chunk = _query_chunk(x.shape[1])
            q_t, k, v_t, g_b, u = _proj_real(x, lw, real_tabs[g], q_chunk)
            k_meta = jnp.pad(jnp.transpose(km_g[g], (1, 0, 2, 3)),
                             ((0, 0), (0, 0), (0, META_PAD - N_META), (0, 0)))
            v_meta_t = jnp.pad(jnp.transpose(vm_g[g], (2, 0, 1, 3)),
                               ((0, 0), (0, 0), (0, 0), (0, META_PAD - N_META)))
            q_meta_t = jnp.pad(jnp.transpose(qm_g[g], (2, 0, 1, 3)),
                               ((0, 0), (0, 0), (0, 0), (0, META_PAD - N_META)))[:, :, None]
            attn = _attention(q_t, k, v_t, k_meta, v_meta_t,
                              q_chunks_per_step=min(x.shape[1], Q_PER_STEP) // q_chunk)
            attn_m = _attention(q_meta_t, k, v_t, k_meta, v_meta_t, q_chunks_per_step=1)
            attn_meta.append(attn_m[:, :N_META])
            um_b = um_g[g]
            zero_row = jnp.zeros_like(um_b[:, :1])
            u_prev_meta.append(jnp.concatenate([zero_row, um_b[:, :-1]], axis=1))
            u_next_meta.append(jnp.concatenate([um_b[:, 1:], u[:, :1]], axis=1))
            new_xs.append(_mix_ffn_real(x, attn, g_b, u, um_b, lw))

        xm = _mix_ffn_meta(xm, _join_meta(attn_meta), gbm, _join_meta(u_prev_meta), um,
                           _join_meta(u_next_meta), lw)
        xs = new_xs

    return (xs[0], xs[1])
```

```python
import functools

import jax
import jax.numpy as jnp
import numpy as np
from jax import lax
from jax.experimental import pallas as pl
from jax.experimental.pallas import tpu as pltpu

F32 = jnp.float32
BF16 = jnp.bfloat16

D_MODEL = 1024
DEPTH = 2
N_META = 16
HEADS = 4
QK_NOPE = 128
QK_ROPE = 64
ROPE_HALF = QK_ROPE // 2
V_HEAD = 128
MLA_WIDTH = HEADS * V_HEAD
CONV_WIDTH = D_MODEL - MLA_WIDTH
CONV_K = 3
Q_LORA = 384
KV_LORA = 256
D_FF = 2816
ROPE_THETA = 10000.0
EPS = 1e-6
ATTN_SCALE = (QK_NOPE + QK_ROPE) ** -0.5
LOG2_E = 1.4426950408889634
Q_SCALE = ATTN_SCALE * LOG2_E

LANES = 128
BF16_SUBLANES = 16
VMEM_LIMIT = 56 * 1024 * 1024

QK_PAD = QK_NOPE + LANES
META_PAD = LANES
ROW_TILE = 512
Q_PER_STEP = 2048
K_CHUNK = 512
K_UNROLL = 8
META_ROWS = 384
FF_CHUNKS = ((0, 1280), (1280, 2816))

C_Q0, C_KV0, C_KR0 = 0, Q_LORA, Q_LORA + KV_LORA
C_GB0 = C_KR0 + LANES
C_GC0 = C_GB0 + CONV_WIDTH
C_CH0 = C_GC0 + CONV_WIDTH
IN_COLS_P = C_CH0 + CONV_WIDTH
Q_HEAD_ROWS = QK_NOPE + QK_ROPE

NEG_BIG = -1e30


def _rms(x, g):
    ms = jnp.mean(x * x, axis=-1, keepdims=True)
    return x * lax.rsqrt(ms + EPS) * g


def _dot(a, b):
    return jnp.dot(a, b, preferred_element_type=F32)


def _dot_nt(a, b):
    return lax.dot_general(a, b, (((1,), (1,)), ((), ())), preferred_element_type=F32)


def _proj_body(x, g_pre, w_in_ref, g_q, wq_t_ref, g_kv, wk_ref, wv_t_ref, cos, sin, cos_t, sin_t):
    h = _rms(x, g_pre).astype(BF16)
    z = _dot(h, w_in_ref[...])
    cqn = _rms(z[:, C_Q0:C_KV0], g_q).astype(BF16)
    ckvn = _rms(z[:, C_KV0:C_KR0], g_kv).astype(BF16)
    k_r = z[:, C_KR0:C_GB0]
    lane = lax.broadcasted_iota(jnp.int32, k_r.shape, 1)
    partner = jnp.where(lane < ROPE_HALF, pltpu.roll(k_r, LANES - ROPE_HALF, 1),
                        pltpu.roll(k_r, ROPE_HALF, 1))
    k_rot = k_r * cos + partner * sin
    g_b = z[:, C_GB0:C_GC0]
    u = z[:, C_GC0:C_CH0] * z[:, C_CH0:IN_COLS_P]
    q_t = _dot_nt(wq_t_ref[...], cqn)
    k_nope = _dot(ckvn, wk_ref[...])
    v_t = _dot_nt(wv_t_ref[...], ckvn)
    q_heads = []
    for hd in range(HEADS):
        r0 = hd * Q_HEAD_ROWS
        nope = q_t[r0:r0 + QK_NOPE]
        x1 = q_t[r0 + QK_NOPE:r0 + QK_NOPE + ROPE_HALF]
        x2 = q_t[r0 + QK_NOPE + ROPE_HALF:r0 + Q_HEAD_ROWS]
        rot = jnp.concatenate([x1 * cos_t - x2 * sin_t, x2 * cos_t + x1 * sin_t,
                               jnp.zeros((QK_PAD - Q_HEAD_ROWS, q_t.shape[1]), F32)], axis=0)
        q_heads.append(((nope * Q_SCALE).astype(BF16), (rot * Q_SCALE).astype(BF16)))
    return q_heads, k_nope.astype(BF16), k_rot.astype(BF16), v_t.astype(BF16), g_b.astype(BF16), u.astype(BF16)


def _proj_kernel(x_ref, g_pre_ref, w_in_ref, g_q_ref, wq_t_ref, g_kv_ref, wk_ref, wv_t_ref,
                 cos_ref, sin_ref, cos_t_ref, sin_t_ref,
                 q_ref, k_ref, v_ref, gb_ref, u_ref):
    q_heads, k_nope, k_rot, v_t, g_b, u = _proj_body(
        x_ref[0], g_pre_ref[...], w_in_ref, g_q_ref[...], wq_t_ref, g_kv_ref[...], wk_ref,
        wv_t_ref, cos_ref[...], sin_ref[...], cos_t_ref[...], sin_t_ref[...])
    n_k = ROW_TILE // K_CHUNK
    for hd in range(HEADS):
        nope, rot = q_heads[hd]
        q_ref[0, hd, 0, :QK_NOPE, :] = nope
        q_ref[0, hd, 0, QK_NOPE:, :] = rot
        k_ref[0, hd, :, :QK_NOPE] = k_nope[:, hd * QK_NOPE:(hd + 1) * QK_NOPE]
        k_ref[0, hd, :, QK_NOPE:] = k_rot
        for c in range(n_k):
            v_ref[0, hd, c] = v_t[hd * V_HEAD:(hd + 1) * V_HEAD, c * K_CHUNK:(c + 1) * K_CHUNK]
    gb_ref[0] = g_b
    u_ref[0] = u


def _proj_meta_kernel(x_ref, g_pre_ref, w_in_ref, g_q_ref, wq_t_ref, g_kv_ref, wk_ref, wv_t_ref,
                      cos_ref, sin_ref, cos_t_ref, sin_t_ref,
                      q_ref, k_ref, v_ref, gb_ref, u_ref):
    q_heads, k_nope, k_rot, v_t, g_b, u = _proj_body(
        x_ref[...], g_pre_ref[...], w_in_ref, g_q_ref[...], wq_t_ref, g_kv_ref[...], wk_ref,
        wv_t_ref, cos_ref[...], sin_ref[...], cos_t_ref[...], sin_t_ref[...])
    for hd in range(HEADS):
        nope, rot = q_heads[hd]
        q_ref[hd, :QK_NOPE, :] = nope
        q_ref[hd, QK_NOPE:, :] = rot
        k_ref[hd, :, :QK_NOPE] = k_nope[:, hd * QK_NOPE:(hd + 1) * QK_NOPE]
        k_ref[hd, :, QK_NOPE:] = k_rot
        v_ref[hd] = v_t[hd * V_HEAD:(hd + 1) * V_HEAD, :]
    gb_ref[...] = g_b
    u_ref[...] = u


def _const_spec(shape):
    zeros = (0,) * len(shape)
    return pl.BlockSpec(shape, lambda *_: zeros, pipeline_mode=pl.Buffered(1))


def _layer_spec(shape, layer):
    zeros = (0,) * len(shape)
    return pl.BlockSpec((None,) + shape, lambda *_: (layer,) + zeros, pipeline_mode=pl.Buffered(1))


def _proj_weight_specs(layer):
    return [
        _layer_spec((1, D_MODEL), layer),
        _layer_spec((D_MODEL, IN_COLS_P), layer),
        _layer_spec((1, Q_LORA), layer),
        _layer_spec((HEADS * Q_HEAD_ROWS, Q_LORA), layer),
        _layer_spec((1, KV_LORA), layer),
        _layer_spec((KV_LORA, HEADS * QK_NOPE), layer),
        _layer_spec((HEADS * V_HEAD, KV_LORA), layer),
    ]


def _proj_real(x, lw, layer, tabs, q_chunk):
    b, seq, _ = x.shape
    nt = seq // ROW_TILE
    cos, sin, cos_t, sin_t = tabs
    in_specs = ([pl.BlockSpec((1, ROW_TILE, D_MODEL), lambda i, j: (i, j, 0))]
                + _proj_weight_specs(layer)
                + [pl.BlockSpec((ROW_TILE, LANES), lambda i, j: (j, 0)),
                   pl.BlockSpec((ROW_TILE, LANES), lambda i, j: (j, 0)),
                   pl.BlockSpec((ROPE_HALF, ROW_TILE), lambda i, j: (0, j)),
                   pl.BlockSpec((ROPE_HALF, ROW_TILE), lambda i, j: (0, j))])
    tiles_per_q = q_chunk // ROW_TILE
    n_k = ROW_TILE // K_CHUNK
    out_shape = [
        jax.ShapeDtypeStruct((b, HEADS, seq // q_chunk, QK_PAD, q_chunk), BF16),
        jax.ShapeDtypeStruct((b, HEADS, seq, QK_PAD), BF16),
        jax.ShapeDtypeStruct((b, HEADS, seq // K_CHUNK, V_HEAD, K_CHUNK), BF16),
        jax.ShapeDtypeStruct((b, seq, CONV_WIDTH), BF16),
        jax.ShapeDtypeStruct((b, seq, CONV_WIDTH), BF16),
    ]
    out_specs = [
        pl.BlockSpec((1, HEADS, 1, QK_PAD, ROW_TILE),
                     lambda i, j: (i, 0, j // tiles_per_q, 0, j % tiles_per_q)),
        pl.BlockSpec((1, HEADS, ROW_TILE, QK_PAD), lambda i, j: (i, 0, j, 0)),
        pl.BlockSpec((1, HEADS, n_k, V_HEAD, K_CHUNK), lambda i, j: (i, 0, j, 0, 0)),
        pl.BlockSpec((1, ROW_TILE, CONV_WIDTH), lambda i, j: (i, j, 0)),
        pl.BlockSpec((1, ROW_TILE, CONV_WIDTH), lambda i, j: (i, j, 0)),
    ]
    return pl.pallas_call(
        _proj_kernel,
        grid=(b, nt),
        in_specs=in_specs,
        out_specs=out_specs,
        out_shape=out_shape,
        compiler_params=pltpu.CompilerParams(
            dimension_semantics=("parallel", "parallel"), vmem_limit_bytes=VMEM_LIMIT),
        name="proj_real",
    )(x, lw["g_pre"], lw["w_in"], lw["g_q"], lw["wq_t"], lw["g_kv"], lw["wk"], lw["wv_t"],
      cos, sin, cos_t, sin_t)


def _proj_meta(xm, lw, layer, tabs):
    cos, sin, cos_t, sin_t = tabs
    r = META_ROWS
    in_specs = ([_const_spec((r, D_MODEL))] + _proj_weight_specs(layer)
                + [_const_spec((r, LANES)), _const_spec((r, LANES)),
                   _const_spec((ROPE_HALF, r)), _const_spec((ROPE_HALF, r))])
    out_shape = [
        jax.ShapeDtypeStruct((HEADS, QK_PAD, r), BF16),
        jax.ShapeDtypeStruct((HEADS, r, QK_PAD), BF16),
        jax.ShapeDtypeStruct((HEADS, V_HEAD, r), BF16),
        jax.ShapeDtypeStruct((r, CONV_WIDTH), BF16),
        jax.ShapeDtypeStruct((r, CONV_WIDTH), BF16),
    ]
    out_specs = [
        pl.BlockSpec((HEADS, QK_PAD, r), lambda i: (0, 0, 0)),
        pl.BlockSpec((HEADS, r, QK_PAD), lambda i: (0, 0, 0)),
        pl.BlockSpec((HEADS, V_HEAD, r), lambda i: (0, 0, 0)),
        pl.BlockSpec((r, CONV_WIDTH), lambda i: (0, 0)),
        pl.BlockSpec((r, CONV_WIDTH), lambda i: (0, 0)),
    ]
    return pl.pallas_call(
        _proj_meta_kernel,
        grid=(1,),
        in_specs=in_specs,
        out_specs=out_specs,
        out_shape=out_shape,
        compiler_params=pltpu.CompilerParams(
            dimension_semantics=("arbitrary",), vmem_limit_bytes=VMEM_LIMIT),
        name="proj_meta",
    )(xm, lw["g_pre"], lw["w_in"], lw["g_q"], lw["wq_t"], lw["g_kv"], lw["wk"], lw["wv_t"],
      cos, sin, cos_t, sin_t)


def _attn_kernel(q_ref, k_ref, v_ref, km_ref, vm_ref, o_ref, s_ref,
                 *, n_qc, n_kc, q_chunk, unroll):
    meta_valid = lax.broadcasted_iota(jnp.int32, (META_PAD, q_chunk), 0) < N_META
    n_groups = n_kc // unroll

    def ones_rows(n):
        first = lax.broadcasted_iota(jnp.int32, (BF16_SUBLANES, n), 0) == 0
        return jnp.where(first, 1.0, 0.0).astype(BF16)

    def put_scores(slot, kc, qc):
        k = k_ref[0, 0, pl.ds(pl.multiple_of(kc * K_CHUNK, K_CHUNK), K_CHUNK), :]
        s = _dot(k, q_ref[0, 0, qc])
        s_ref[slot] = s
        return jnp.max(s, axis=0, keepdims=True)

    def with_ones(v_t):
        return jnp.concatenate([v_t, ones_rows(v_t.shape[1])], axis=0)

    def accumulate(s, s_max, v_t, m, acc):
        m_new = jnp.maximum(m, s_max)
        p = jnp.exp2(s - m_new).astype(BF16)
        return m_new, jnp.exp2(m - m_new) * acc + _dot(with_ones(v_t), p)

    def k_group(base, qc, q_next, carry, last):
        m, acc, s_max = carry
        for t in range(unroll):
            final = last and t == unroll - 1
            if not final:
                next_max = put_scores((t + 1) % 2, base + t + 1, qc)
            elif q_next is not None:
                next_max = put_scores(0, 0, q_next)
            else:
                next_max = s_max
            m, acc = accumulate(s_ref[t % 2], s_max, v_ref[0, 0, base + t], m, acc)
            if last and t == 0:
                s_meta = jnp.where(meta_valid, _dot(km_ref[0, 0], q_ref[0, 0, qc]), NEG_BIG)
                m, acc = accumulate(s_meta, jnp.max(s_meta, axis=0, keepdims=True), vm_ref[0, 0], m, acc)
            s_max = next_max
        return m, acc, s_max

    def q_chunk_body(qc, q_next, s_max):
        m0 = jnp.full((1, q_chunk), NEG_BIG, F32)
        acc0 = jnp.zeros((V_HEAD + BF16_SUBLANES, q_chunk), F32)
        carry = (m0, acc0, s_max)
        if n_groups > 1:
            carry = lax.fori_loop(
                0, n_groups - 1, lambda g, c: k_group(g * unroll, qc, q_next, c, False), carry)
        _, acc, s_max = k_group((n_groups - 1) * unroll, qc, q_next, carry, True)
        out = (acc[:V_HEAD] / acc[V_HEAD:V_HEAD + 1]).T
        o_ref[0, pl.ds(pl.multiple_of(qc * q_chunk, q_chunk), q_chunk), :] = out.astype(o_ref.dtype)
        return s_max

    s_max = put_scores(0, 0, 0)
    if n_qc == 1:
        q_chunk_body(0, None, s_max)
    else:
        lax.fori_loop(
            0, n_qc, lambda qc, s_max: q_chunk_body(qc, jnp.minimum(qc + 1, n_qc - 1), s_max), s_max)


def _attention(q_t, k, v_t, k_meta, v_meta_t, *, q_chunks_per_step):
    b, _, n_q_total, _, q_chunk = q_t.shape
    seq_k = k.shape[2]
    n_kc = seq_k // K_CHUNK
    unroll = min(K_UNROLL, n_kc)
    assert unroll % 2 == 0 and n_kc % unroll == 0
    n_qc = q_chunks_per_step
    n_steps = n_q_total // n_qc
    tq = n_qc * q_chunk
    kern = functools.partial(_attn_kernel, n_qc=n_qc, n_kc=n_kc, q_chunk=q_chunk, unroll=unroll)
    return pl.pallas_call(
        kern,
        grid=(b, HEADS, n_steps),
        in_specs=[
            pl.BlockSpec((1, 1, n_qc, QK_PAD, q_chunk), lambda i, h, j: (i, h, j, 0, 0)),
            pl.BlockSpec((1, 1, seq_k, QK_PAD), lambda i, h, j: (i, h, 0, 0)),
            pl.BlockSpec((1, 1, n_kc, V_HEAD, K_CHUNK), lambda i, h, j: (i, h, 0, 0, 0)),
            pl.BlockSpec((1, 1, META_PAD, QK_PAD), lambda i, h, j: (i, h, 0, 0)),
            pl.BlockSpec((1, 1, V_HEAD, META_PAD), lambda i, h, j: (i, h, 0, 0)),
        ],
        out_specs=pl.BlockSpec((1, tq, V_HEAD), lambda i, h, j: (i, j, h)),
        out_shape=jax.ShapeDtypeStruct((b, n_q_total * q_chunk, MLA_WIDTH), BF16),
        scratch_shapes=[pltpu.VMEM((2, K_CHUNK, q_chunk), F32)],
        compiler_params=pltpu.CompilerParams(
            dimension_semantics=("parallel", "parallel", "parallel"), vmem_limit_bytes=VMEM_LIMIT),
        name="attention",
    )(q_t, k, v_t, k_meta, v_meta_t)


def _mix_ffn_body(x, attn, g_b, u_prev, u_cur, u_next, conv_w, w_out_ref, g_post_mix, g_pre_ffn,
                  w_gate_ref, w_up_ref, w_down_ref, g_post_ffn):
    y = u_prev * conv_w[0:1] + u_cur * conv_w[1:2] + u_next * conv_w[2:3]
    conv = (g_b.astype(F32) * y).astype(BF16)
    mix = _dot(attn, w_out_ref[:MLA_WIDTH, :]) + _dot(conv, w_out_ref[MLA_WIDTH:, :])
    x1 = x + _rms(mix, g_post_mix)
    h = _rms(x1, g_pre_ffn).astype(BF16)
    f = None
    for c0, c1 in FF_CHUNKS:
        gate = _dot(h, w_gate_ref[:, c0:c1])
        up = _dot(h, w_up_ref[:, c0:c1])
        act = (gate / (1.0 + jnp.exp(-gate)) * up).astype(BF16)
        part = _dot(act, w_down_ref[c0:c1, :])
        f = part if f is None else f + part
    return x1 + _rms(f, g_post_ffn)


def _mix_ffn_kernel(x_ref, attn_ref, gb_ref, u_ref, u_before_ref, u_after_ref, u_meta_ref,
                    conv_w_ref, w_out_ref, g_post_mix_ref, g_pre_ffn_ref,
                    w_gate_ref, w_up_ref, w_down_ref, g_post_ffn_ref, o_ref):
    j = pl.program_id(1)
    last = pl.num_programs(1) - 1
    u_cur = u_ref[0].astype(F32)
    tail = BF16_SUBLANES - 1
    before = jnp.where(j == 0, u_meta_ref[0, tail:tail + 1, :], u_before_ref[0, tail:tail + 1, :])
    after = jnp.where(j == last, jnp.zeros_like(u_after_ref[0, 0:1, :]), u_after_ref[0, 0:1, :])
    row = lax.broadcasted_iota(jnp.int32, u_cur.shape, 0)
    u_prev = jnp.where(row == 0, before.astype(F32), pltpu.roll(u_cur, 1, 0))
    u_next = jnp.where(row == ROW_TILE - 1, after.astype(F32), pltpu.roll(u_cur, ROW_TILE - 1, 0))
    o_ref[0] = _mix_ffn_body(
        x_ref[0], attn_ref[0], gb_ref[0], u_prev, u_cur, u_next, conv_w_ref[...], w_out_ref,
        g_post_mix_ref[...], g_pre_ffn_ref[...], w_gate_ref, w_up_ref, w_down_ref,
        g_post_ffn_ref[...])


def _mix_ffn_meta_kernel(x_ref, attn_ref, gb_ref, u_prev_ref, u_ref, u_next_ref,
                         conv_w_ref, w_out_ref, g_post_mix_ref, g_pre_ffn_ref,
                         w_gate_ref, w_up_ref, w_down_ref, g_post_ffn_ref, o_ref):
    o_ref[...] = _mix_ffn_body(
        x_ref[...], attn_ref[...], gb_ref[...], u_prev_ref[...].astype(F32),
        u_ref[...].astype(F32), u_next_ref[...].astype(F32), conv_w_ref[...], w_out_ref,
        g_post_mix_ref[...], g_pre_ffn_ref[...], w_gate_ref, w_up_ref, w_down_ref,
        g_post_ffn_ref[...])


def _ffn_weight_specs(layer):
    return [
        _layer_spec((CONV_K, CONV_WIDTH), layer),
        _layer_spec((D_MODEL, D_MODEL), layer),
        _layer_spec((1, D_MODEL), layer),
        _layer_spec((1, D_MODEL), layer),
        _layer_spec((D_MODEL, D_FF), layer),
        _layer_spec((D_MODEL, D_FF), layer),
        _layer_spec((D_FF, D_MODEL), layer),
        _layer_spec((1, D_MODEL), layer),
    ]


def _ffn_weights(lw):
    return (lw["conv_w"], lw["w_out"], lw["g_post_mix"], lw["g_pre_ffn"], lw["w_gate"], lw["w_up"],
            lw["w_down"], lw["g_post_ffn"])


def _mix_ffn_real(x, attn, g_b, u, u_meta, lw, layer):
    b, seq, _ = x.shape
    nt = seq // ROW_TILE
    halo_per_tile = ROW_TILE // BF16_SUBLANES
    n_halo = seq // BF16_SUBLANES
    tile = lambda w: pl.BlockSpec((1, ROW_TILE, w), lambda i, j: (i, j, 0))
    in_specs = [
        tile(D_MODEL), tile(MLA_WIDTH), tile(CONV_WIDTH), tile(CONV_WIDTH),
        pl.BlockSpec((1, BF16_SUBLANES, CONV_WIDTH),
                     lambda i, j: (i, jnp.maximum(j * halo_per_tile - 1, 0), 0)),
        pl.BlockSpec((1, BF16_SUBLANES, CONV_WIDTH),
                     lambda i, j: (i, jnp.minimum((j + 1) * halo_per_tile, n_halo - 1), 0)),
        pl.BlockSpec((1, BF16_SUBLANES, CONV_WIDTH), lambda i, j: (i, 0, 0)),
    ] + _ffn_weight_specs(layer)
    return pl.pallas_call(
        _mix_ffn_kernel,
        grid=(b, nt),
        in_specs=in_specs,
        out_specs=tile(D_MODEL),
        out_shape=jax.ShapeDtypeStruct((b, seq, D_MODEL), F32),
        compiler_params=pltpu.CompilerParams(
            dimension_semantics=("parallel", "parallel"), vmem_limit_bytes=VMEM_LIMIT),
        name="mix_ffn_real",
    )(x, attn, g_b, u, u, u, u_meta, *_ffn_weights(lw))


def _mix_ffn_meta(xm, attn, g_b, u_prev, u, u_next, lw, layer):
    r = META_ROWS
    in_specs = [_const_spec((r, D_MODEL)), _const_spec((r, MLA_WIDTH))] \
        + [_const_spec((r, CONV_WIDTH)) for _ in range(4)] + _ffn_weight_specs(layer)
    return pl.pallas_call(
        _mix_ffn_meta_kernel,
        grid=(1,),
        in_specs=in_specs,
        out_specs=pl.BlockSpec((r, D_MODEL), lambda i: (0, 0)),
        out_shape=jax.ShapeDtypeStruct((r, D_MODEL), F32),
        compiler_params=pltpu.CompilerParams(
            dimension_semantics=("arbitrary",), vmem_limit_bytes=VMEM_LIMIT),
        name="mix_ffn_meta",
    )(xm, attn, g_b, u_prev, u, u_next, *_ffn_weights(lw))


def _pack_weights(pre_mix_g, w_in, q_norm_g, w_q_up, kv_norm_g, w_kv_up, conv_w, w_out,
                  post_mix_g, pre_ffn_g, w_gate, w_up, w_down, post_ffn_g):
    i2 = Q_LORA + KV_LORA + QK_ROPE
    zpad = jnp.zeros((DEPTH, D_MODEL, LANES - QK_ROPE), F32)
    w_in_p = jnp.concatenate([w_in[:, :, :i2], zpad, w_in[:, :, i2:]], axis=2).astype(BF16)
    wq_t = jnp.swapaxes(w_q_up, 1, 2).astype(BF16)
    wkv = w_kv_up.reshape(DEPTH, KV_LORA, HEADS, QK_NOPE + V_HEAD)
    wk = wkv[..., :QK_NOPE].reshape(DEPTH, KV_LORA, HEADS * QK_NOPE).astype(BF16)
    wv_t = jnp.swapaxes(wkv[..., QK_NOPE:].reshape(DEPTH, KV_LORA, HEADS * V_HEAD), 1, 2).astype(BF16)
    row = lambda g: g.reshape(DEPTH, 1, -1).astype(F32)
    return dict(
        g_pre=row(pre_mix_g), w_in=w_in_p, g_q=row(q_norm_g), wq_t=wq_t, g_kv=row(kv_norm_g),
        wk=wk, wv_t=wv_t, conv_w=conv_w.astype(F32), w_out=w_out.astype(BF16),
        g_post_mix=row(post_mix_g), g_pre_ffn=row(pre_ffn_g), w_gate=w_gate.astype(BF16),
        w_up=w_up.astype(BF16), w_down=w_down.astype(BF16), g_post_ffn=row(post_ffn_g))


def _rope_tables(pos):
    inv_freq = np.float32(ROPE_THETA) ** (-np.arange(0, QK_ROPE, 2, dtype=np.float32) / np.float32(QK_ROPE))
    ang = (pos.astype(np.float32)[:, None] * inv_freq.astype(np.float32)[None, :]).astype(np.float64)
    c, s = np.cos(ang).astype(np.float32), np.sin(ang).astype(np.float32)
    z = np.zeros((pos.shape[0], LANES - QK_ROPE), np.float32)
    cos = np.concatenate([c, c, z], axis=1)
    sin = np.concatenate([-s, s, z], axis=1)
    return tuple(jnp.asarray(t) for t in (cos, sin, np.ascontiguousarray(c.T), np.ascontiguousarray(s.T)))


def _query_chunk(seq):
    return 1024 if seq // K_CHUNK <= K_UNROLL else 512


def _meta_to_groups(a, batches, axis):
    out, r0 = [], 0
    for b in batches:
        sl = lax.slice_in_dim(a, r0, r0 + b * N_META, axis=axis)
        shape = a.shape[:axis] + (b, N_META) + a.shape[axis + 1:]
        out.append(sl.reshape(shape))
        r0 += b * N_META
    return out


def _join_meta(parts):
    flat = [p.reshape(-1, p.shape[-1]) for p in parts]
    rows = sum(f.shape[0] for f in flat)
    flat.append(jnp.zeros((META_ROWS - rows, flat[0].shape[-1]), flat[0].dtype))
    return jnp.concatenate(flat, axis=0)


def kernel(x_prompt, x_sample, meta_tokens, pre_mix_g, w_in, q_norm_g, w_q_up, kv_norm_g, w_kv_up,
           conv_w, w_out, post_mix_g, pre_ffn_g, w_gate, w_up, w_down, post_ffn_g):
    xs = [x_prompt, x_sample]
    batches = [x.shape[0] for x in xs]
    assert sum(batches) * N_META <= META_ROWS
    for x in xs:
        assert x.shape[1] % ROW_TILE == 0 and x.shape[2] == D_MODEL

    real_tabs = [_rope_tables(N_META + np.arange(x.shape[1])) for x in xs]
    meta_pos = np.arange(META_ROWS) % N_META
    meta_tabs = _rope_tables(meta_pos)

    xm = _join_meta([jnp.broadcast_to(meta_tokens.astype(F32)[None], (b, N_META, D_MODEL))
                     for b in batches])

    lw = _pack_weights(pre_mix_g, w_in, q_norm_g, w_q_up, kv_norm_g, w_kv_up, conv_w, w_out,
                       post_mix_g, pre_ffn_g, w_gate, w_up, w_down, post_ffn_g)
    for layer in range(DEPTH):
        qm, km, vm, gbm, um = _proj_meta(xm, lw, layer, meta_tabs)
        qm_g = _meta_to_groups(qm, batches, 2)
        km_g = _meta_to_groups(km, batches, 1)
        vm_g = _meta_to_groups(vm, batches, 2)
        um_g = _meta_to_groups(um, batches, 0)

        new_xs, attn_meta, u_prev_meta, u_next_meta = [], [], [], []
        for g, x in enumerate(xs):
            q_chunk = _query_chunk(x.shape[1])
            q_t, k, v_t, g_b, u = _proj_real(x, lw, layer, real_tabs[g], q_chunk)
            k_meta = jnp.pad(jnp.transpose(km_g[g], (1, 0, 2, 3)),
                             ((0, 0), (0, 0), (0, META_PAD - N_META), (0, 0)))
            v_meta_t = jnp.pad(jnp.transpose(vm_g[g], (2, 0, 1, 3)),
                               ((0, 0), (0, 0), (0, 0), (0, META_PAD - N_META)))
            q_meta_t = jnp.pad(jnp.transpose(qm_g[g], (2, 0, 1, 3)),
                               ((0, 0), (0, 0), (0, 0), (0, META_PAD - N_META)))[:, :, None]
            attn = _attention(q_t, k, v_t, k_meta, v_meta_t,
                              q_chunks_per_step=min(x.shape[1], Q_PER_STEP) // q_chunk)
            attn_m = _attention(q_meta_t, k, v_t, k_meta, v_meta_t, q_chunks_per_step=1)
            attn_meta.append(attn_m[:, :N_META])
            um_b = um_g[g]
            zero_row = jnp.zeros_like(um_b[:, :1])
            u_prev_meta.append(jnp.concatenate([zero_row, um_b[:, :-1]], axis=1))
            u_next_meta.append(jnp.concatenate([um_b[:, 1:], u[:, :1]], axis=1))
            new_xs.append(_mix_ffn_real(x, attn, g_b, u, um_b, lw, layer))

        xm = _mix_ffn_meta(xm, _join_meta(attn_meta), gbm, _join_meta(u_prev_meta), um,
                           _join_meta(u_next_meta), lw, layer)
        xs = new_xs

    return (xs[0], xs[1])
```

```python
import functools

import jax
import jax.numpy as jnp
import numpy as np
from jax import lax
from jax.experimental import pallas as pl
from jax.experimental.pallas import tpu as pltpu

F32 = jnp.float32
BF16 = jnp.bfloat16

D_MODEL = 1024
DEPTH = 2
N_META = 16
HEADS = 4
QK_NOPE = 128
QK_ROPE = 64
ROPE_HALF = QK_ROPE // 2
V_HEAD = 128
MLA_WIDTH = HEADS * V_HEAD
CONV_WIDTH = D_MODEL - MLA_WIDTH
CONV_K = 3
Q_LORA = 384
KV_LORA = 256
D_FF = 2816
ROPE_THETA = 10000.0
EPS = 1e-6
ATTN_SCALE = (QK_NOPE + QK_ROPE) ** -0.5
LOG2_E = 1.4426950408889634
Q_SCALE = ATTN_SCALE * LOG2_E

LANES = 128
BF16_SUBLANES = 16
VMEM_LIMIT = 56 * 1024 * 1024

QK_PAD = QK_NOPE + LANES
META_PAD = LANES
ROW_TILE = 512
Q_PER_STEP = 4096
K_CHUNK = 512
K_UNROLL = 16
META_ROWS = 384
FF_CHUNKS = ((0, 1280), (1280, 2816))

C_Q0, C_KV0, C_KR0 = 0, Q_LORA, Q_LORA + KV_LORA
C_GB0 = C_KR0 + LANES
C_GC0 = C_GB0 + CONV_WIDTH
C_CH0 = C_GC0 + CONV_WIDTH
IN_COLS_P = C_CH0 + CONV_WIDTH
Q_HEAD_ROWS = QK_NOPE + QK_ROPE

NEG_BIG = -1e30


def _rms(x, g):
    ms = jnp.mean(x * x, axis=-1, keepdims=True)
    return x * lax.rsqrt(ms + EPS) * g


def _dot(a, b):
    return jnp.dot(a, b, preferred_element_type=F32)


def _dot_nt(a, b):
    return lax.dot_general(a, b, (((1,), (1,)), ((), ())), preferred_element_type=F32)


def _proj_body(x, g_pre, w_in_ref, g_q, wq_t_ref, g_kv, wk_ref, wv_t_ref, cos, sin, cos_t, sin_t):
    h = _rms(x, g_pre).astype(BF16)
    z = _dot(h, w_in_ref[...])
    cqn = _rms(z[:, C_Q0:C_KV0], g_q).astype(BF16)
    ckvn = _rms(z[:, C_KV0:C_KR0], g_kv).astype(BF16)
    k_r = z[:, C_KR0:C_GB0]
    lane = lax.broadcasted_iota(jnp.int32, k_r.shape, 1)
    partner = jnp.where(lane < ROPE_HALF, pltpu.roll(k_r, LANES - ROPE_HALF, 1),
                        pltpu.roll(k_r, ROPE_HALF, 1))
    k_rot = k_r * cos + partner * sin
    g_b = z[:, C_GB0:C_GC0]
    u = z[:, C_GC0:C_CH0] * z[:, C_CH0:IN_COLS_P]
    q_t = _dot_nt(wq_t_ref[...], cqn)
    k_nope = _dot(ckvn, wk_ref[...])
    v_t = _dot_nt(wv_t_ref[...], ckvn)
    q_heads = []
    for hd in range(HEADS):
        r0 = hd * Q_HEAD_ROWS
        nope = q_t[r0:r0 + QK_NOPE]
        x1 = q_t[r0 + QK_NOPE:r0 + QK_NOPE + ROPE_HALF]
        x2 = q_t[r0 + QK_NOPE + ROPE_HALF:r0 + Q_HEAD_ROWS]
        rot = jnp.concatenate([x1 * cos_t - x2 * sin_t, x2 * cos_t + x1 * sin_t,
                               jnp.zeros((QK_PAD - Q_HEAD_ROWS, q_t.shape[1]), F32)], axis=0)
        q_heads.append(((nope * Q_SCALE).astype(BF16), (rot * Q_SCALE).astype(BF16)))
    return q_heads, k_nope.astype(BF16), k_rot.astype(BF16), v_t.astype(BF16), g_b.astype(BF16), u.astype(BF16)


def _proj_kernel(x_ref, g_pre_ref, w_in_ref, g_q_ref, wq_t_ref, g_kv_ref, wk_ref, wv_t_ref,
                 cos_ref, sin_ref, cos_t_ref, sin_t_ref,
                 q_ref, k_ref, v_ref, gb_ref, u_ref):
    q_heads, k_nope, k_rot, v_t, g_b, u = _proj_body(
        x_ref[0], g_pre_ref[...], w_in_ref, g_q_ref[...], wq_t_ref, g_kv_ref[...], wk_ref,
        wv_t_ref, cos_ref[...], sin_ref[...], cos_t_ref[...], sin_t_ref[...])
    n_k = ROW_TILE // K_CHUNK
    for hd in range(HEADS):
        nope, rot = q_heads[hd]
        q_ref[0, hd, 0, :QK_NOPE, :] = nope
        q_ref[0, hd, 0, QK_NOPE:, :] = rot
        k_ref[0, hd, :, :QK_NOPE] = k_nope[:, hd * QK_NOPE:(hd + 1) * QK_NOPE]
        k_ref[0, hd, :, QK_NOPE:] = k_rot
        for c in range(n_k):
            v_ref[0, hd, c] = v_t[hd * V_HEAD:(hd + 1) * V_HEAD, c * K_CHUNK:(c + 1) * K_CHUNK]
    gb_ref[0] = g_b
    u_ref[0] = u


def _proj_meta_kernel(x_ref, g_pre_ref, w_in_ref, g_q_ref, wq_t_ref, g_kv_ref, wk_ref, wv_t_ref,
                      cos_ref, sin_ref, cos_t_ref, sin_t_ref,
                      q_ref, k_ref, v_ref, gb_ref, u_ref):
    q_heads, k_nope, k_rot, v_t, g_b, u = _proj_body(
        x_ref[...], g_pre_ref[...], w_in_ref, g_q_ref[...], wq_t_ref, g_kv_ref[...], wk_ref,
        wv_t_ref, cos_ref[...], sin_ref[...], cos_t_ref[...], sin_t_ref[...])
    for hd in range(HEADS):
        nope, rot = q_heads[hd]
        q_ref[hd, :QK_NOPE, :] = nope
        q_ref[hd, QK_NOPE:, :] = rot
        k_ref[hd, :, :QK_NOPE] = k_nope[:, hd * QK_NOPE:(hd + 1) * QK_NOPE]
        k_ref[hd, :, QK_NOPE:] = k_rot
        v_ref[hd] = v_t[hd * V_HEAD:(hd + 1) * V_HEAD, :]
    gb_ref[...] = g_b
    u_ref[...] = u


def _const_spec(shape):
    zeros = (0,) * len(shape)
    return pl.BlockSpec(shape, lambda *_: zeros, pipeline_mode=pl.Buffered(1))


def _layer_spec(shape, layer):
    zeros = (0,) * len(shape)
    return pl.BlockSpec((None,) + shape, lambda *_: (layer,) + zeros, pipeline_mode=pl.Buffered(1))


def _proj_weight_specs(layer):
    return [
        _layer_spec((1, D_MODEL), layer),
        _layer_spec((D_MODEL, IN_COLS_P), layer),
        _layer_spec((1, Q_LORA), layer),
        _layer_spec((HEADS * Q_HEAD_ROWS, Q_LORA), layer),
        _layer_spec((1, KV_LORA), layer),
        _layer_spec((KV_LORA, HEADS * QK_NOPE), layer),
        _layer_spec((HEADS * V_HEAD, KV_LORA), layer),
    ]


def _proj_real(x, lw, layer, tabs, q_chunk):
    b, seq, _ = x.shape
    nt = seq // ROW_TILE
    cos, sin, cos_t, sin_t = tabs
    in_specs = ([pl.BlockSpec((1, ROW_TILE, D_MODEL), lambda i, j: (i, j, 0))]
                + _proj_weight_specs(layer)
                + [pl.BlockSpec((ROW_TILE, LANES), lambda i, j: (j, 0)),
                   pl.BlockSpec((ROW_TILE, LANES), lambda i, j: (j, 0)),
                   pl.BlockSpec((ROPE_HALF, ROW_TILE), lambda i, j: (0, j)),
                   pl.BlockSpec((ROPE_HALF, ROW_TILE), lambda i, j: (0, j))])
    tiles_per_q = q_chunk // ROW_TILE
    n_k = ROW_TILE // K_CHUNK
    out_shape = [
        jax.ShapeDtypeStruct((b, HEADS, seq // q_chunk, QK_PAD, q_chunk), BF16),
        jax.ShapeDtypeStruct((b, HEADS, seq, QK_PAD), BF16),
        jax.ShapeDtypeStruct((b, HEADS, seq // K_CHUNK, V_HEAD, K_CHUNK), BF16),
        jax.ShapeDtypeStruct((b, seq, CONV_WIDTH), BF16),
        jax.ShapeDtypeStruct((b, seq, CONV_WIDTH), BF16),
    ]
    out_specs = [
        pl.BlockSpec((1, HEADS, 1, QK_PAD, ROW_TILE),
                     lambda i, j: (i, 0, j // tiles_per_q, 0, j % tiles_per_q)),
        pl.BlockSpec((1, HEADS, ROW_TILE, QK_PAD), lambda i, j: (i, 0, j, 0)),
        pl.BlockSpec((1, HEADS, n_k, V_HEAD, K_CHUNK), lambda i, j: (i, 0, j, 0, 0)),
        pl.BlockSpec((1, ROW_TILE, CONV_WIDTH), lambda i, j: (i, j, 0)),
        pl.BlockSpec((1, ROW_TILE, CONV_WIDTH), lambda i, j: (i, j, 0)),
    ]
    return pl.pallas_call(
        _proj_kernel,
        grid=(b, nt),
        in_specs=in_specs,
        out_specs=out_specs,
        out_shape=out_shape,
        compiler_params=pltpu.CompilerParams(
            dimension_semantics=("parallel", "parallel"), vmem_limit_bytes=VMEM_LIMIT),
        name="proj_real",
    )(x, lw["g_pre"], lw["w_in"], lw["g_q"], lw["wq_t"], lw["g_kv"], lw["wk"], lw["wv_t"],
      cos, sin, cos_t, sin_t)


def _proj_meta(xm, lw, layer, tabs):
    cos, sin, cos_t, sin_t = tabs
    r = META_ROWS
    in_specs = ([_const_spec((r, D_MODEL))] + _proj_weight_specs(layer)
                + [_const_spec((r, LANES)), _const_spec((r, LANES)),
                   _const_spec((ROPE_HALF, r)), _const_spec((ROPE_HALF, r))])
    out_shape = [
        jax.ShapeDtypeStruct((HEADS, QK_PAD, r), BF16),
        jax.ShapeDtypeStruct((HEADS, r, QK_PAD), BF16),
        jax.ShapeDtypeStruct((HEADS, V_HEAD, r), BF16),
        jax.ShapeDtypeStruct((r, CONV_WIDTH), BF16),
        jax.ShapeDtypeStruct((r, CONV_WIDTH), BF16),
    ]
    out_specs = [
        pl.BlockSpec((HEADS, QK_PAD, r), lambda i: (0, 0, 0)),
        pl.BlockSpec((HEADS, r, QK_PAD), lambda i: (0, 0, 0)),
        pl.BlockSpec((HEADS, V_HEAD, r), lambda i: (0, 0, 0)),
        pl.BlockSpec((r, CONV_WIDTH), lambda i: (0, 0)),
        pl.BlockSpec((r, CONV_WIDTH), lambda i: (0, 0)),
    ]
    return pl.pallas_call(
        _proj_meta_kernel,
        grid=(1,),
        in_specs=in_specs,
        out_specs=out_specs,
        out_shape=out_shape,
        compiler_params=pltpu.CompilerParams(
            dimension_semantics=("arbitrary",), vmem_limit_bytes=VMEM_LIMIT),
        name="proj_meta",
    )(xm, lw["g_pre"], lw["w_in"], lw["g_q"], lw["wq_t"], lw["g_kv"], lw["wk"], lw["wv_t"],
      cos, sin, cos_t, sin_t)


def _attn_kernel(q_ref, k_ref, v_ref, km_ref, vm_ref, o_ref, s_ref,
                 *, n_qc, n_kc, q_chunk, unroll):
    meta_valid = lax.broadcasted_iota(jnp.int32, (META_PAD, q_chunk), 0) < N_META
    n_groups = n_kc // unroll

    def ones_rows(n):
        first = lax.broadcasted_iota(jnp.int32, (BF16_SUBLANES, n), 0) == 0
        return jnp.where(first, 1.0, 0.0).astype(BF16)

    def put_scores(slot, kc, qc):
        k = k_ref[0, 0, pl.ds(pl.multiple_of(kc * K_CHUNK, K_CHUNK), K_CHUNK), :]
        s = _dot(k, q_ref[0, 0, qc])
        s_ref[slot] = s
        return jnp.max(s, axis=0, keepdims=True)

    def with_ones(v_t):
        return jnp.concatenate([v_t, ones_rows(v_t.shape[1])], axis=0)

    def accumulate(s, s_max, v_t, m, acc):
        m_new = jnp.maximum(m, s_max)
        p = jnp.exp2(s - m_new).astype(BF16)
        return m_new, jnp.exp2(m - m_new) * acc + _dot(with_ones(v_t), p)

    def k_group(base, qc, q_next, carry, last):
        m, acc, s_max = carry
        for t in range(unroll):
            final = last and t == unroll - 1
            if not final:
                next_max = put_scores((t + 1) % 2, base + t + 1, qc)
            elif q_next is not None:
                next_max = put_scores(0, 0, q_next)
            else:
                next_max = s_max
            m, acc = accumulate(s_ref[t % 2], s_max, v_ref[0, 0, base + t], m, acc)
            if last and t == 0:
                s_meta = jnp.where(meta_valid, _dot(km_ref[0, 0], q_ref[0, 0, qc]), NEG_BIG)
                m, acc = accumulate(s_meta, jnp.max(s_meta, axis=0, keepdims=True), vm_ref[0, 0], m, acc)
            s_max = next_max
        return m, acc, s_max

    def q_chunk_body(qc, q_next, s_max):
        m0 = jnp.full((1, q_chunk), NEG_BIG, F32)
        acc0 = jnp.zeros((V_HEAD + BF16_SUBLANES, q_chunk), F32)
        carry = (m0, acc0, s_max)
        if n_groups > 1:
            carry = lax.fori_loop(
                0, n_groups - 1, lambda g, c: k_group(g * unroll, qc, q_next, c, False), carry)
        _, acc, s_max = k_group((n_groups - 1) * unroll, qc, q_next, carry, True)
        out = (acc[:V_HEAD] / acc[V_HEAD:V_HEAD + 1]).T
        o_ref[0, pl.ds(pl.multiple_of(qc * q_chunk, q_chunk), q_chunk), :] = out.astype(o_ref.dtype)
        return s_max

    s_max = put_scores(0, 0, 0)
    if n_qc == 1:
        q_chunk_body(0, None, s_max)
    else:
        lax.fori_loop(
            0, n_qc, lambda qc, s_max: q_chunk_body(qc, jnp.minimum(qc + 1, n_qc - 1), s_max), s_max)


def _attention(q_t, k, v_t, k_meta, v_meta_t, *, q_chunks_per_step):
    b, _, n_q_total, _, q_chunk = q_t.shape
    seq_k = k.shape[2]
    n_kc = seq_k // K_CHUNK
    unroll = min(K_UNROLL, n_kc)
    assert unroll % 2 == 0 and n_kc % unroll == 0
    n_qc = q_chunks_per_step
    n_steps = n_q_total // n_qc
    tq = n_qc * q_chunk
    kern = functools.partial(_attn_kernel, n_qc=n_qc, n_kc=n_kc, q_chunk=q_chunk, unroll=unroll)
    return pl.pallas_call(
        kern,
        grid=(b, HEADS, n_steps),
        in_specs=[
            pl.BlockSpec((1, 1, n_qc, QK_PAD, q_chunk), lambda i, h, j: (i, h, j, 0, 0)),
            pl.BlockSpec((1, 1, seq_k, QK_PAD), lambda i, h, j: (i, h, 0, 0)),
            pl.BlockSpec((1, 1, n_kc, V_HEAD, K_CHUNK), lambda i, h, j: (i, h, 0, 0, 0)),
            pl.BlockSpec((1, 1, META_PAD, QK_PAD), lambda i, h, j: (i, h, 0, 0)),
            pl.BlockSpec((1, 1, V_HEAD, META_PAD), lambda i, h, j: (i, h, 0, 0)),
        ],
        out_specs=pl.BlockSpec((1, tq, V_HEAD), lambda i, h, j: (i, j, h)),
        out_shape=jax.ShapeDtypeStruct((b, n_q_total * q_chunk, MLA_WIDTH), BF16),
        scratch_shapes=[pltpu.VMEM((2, K_CHUNK, q_chunk), F32)],
        compiler_params=pltpu.CompilerParams(
            dimension_semantics=("parallel", "parallel", "parallel"), vmem_limit_bytes=VMEM_LIMIT),
        name="attention",
    )(q_t, k, v_t, k_meta, v_meta_t)


def _mix_ffn_body(x, attn, g_b, u_prev, u_cur, u_next, conv_w, w_out_ref, g_post_mix, g_pre_ffn,
                  w_gate_ref, w_up_ref, w_down_ref, g_post_ffn):
    y = u_prev * conv_w[0:1] + u_cur * conv_w[1:2] + u_next * conv_w[2:3]
    conv = (g_b.astype(F32) * y).astype(BF16)
    mix = _dot(attn, w_out_ref[:MLA_WIDTH, :]) + _dot(conv, w_out_ref[MLA_WIDTH:, :])
    x1 = x + _rms(mix, g_post_mix)
    h = _rms(x1, g_pre_ffn).astype(BF16)
    f = None
    for c0, c1 in FF_CHUNKS:
        gate = _dot(h, w_gate_ref[:, c0:c1])
        up = _dot(h, w_up_ref[:, c0:c1])
        act = (gate / (1.0 + jnp.exp(-gate)) * up).astype(BF16)
        part = _dot(act, w_down_ref[c0:c1, :])
        f = part if f is None else f + part
    return x1 + _rms(f, g_post_ffn)


def _mix_ffn_kernel(x_ref, attn_ref, gb_ref, u_ref, u_before_ref, u_after_ref, u_meta_ref,
                    conv_w_ref, w_out_ref, g_post_mix_ref, g_pre_ffn_ref,
                    w_gate_ref, w_up_ref, w_down_ref, g_post_ffn_ref, o_ref):
    j = pl.program_id(1)
    last = pl.num_programs(1) - 1
    u_cur = u_ref[0].astype(F32)
    tail = BF16_SUBLANES - 1
    before = jnp.where(j == 0, u_meta_ref[0, tail:tail + 1, :], u_before_ref[0, tail:tail + 1, :])
    after = jnp.where(j == last, jnp.zeros_like(u_after_ref[0, 0:1, :]), u_after_ref[0, 0:1, :])
    row = lax.broadcasted_iota(jnp.int32, u_cur.shape, 0)
    u_prev = jnp.where(row == 0, before.astype(F32), pltpu.roll(u_cur, 1, 0))
    u_next = jnp.where(row == ROW_TILE - 1, after.astype(F32), pltpu.roll(u_cur, ROW_TILE - 1, 0))
    o_ref[0] = _mix_ffn_body(
        x_ref[0], attn_ref[0], gb_ref[0], u_prev, u_cur, u_next, conv_w_ref[...], w_out_ref,
        g_post_mix_ref[...], g_pre_ffn_ref[...], w_gate_ref, w_up_ref, w_down_ref,
        g_post_ffn_ref[...])


def _mix_ffn_meta_kernel(x_ref, attn_ref, gb_ref, u_prev_ref, u_ref, u_next_ref,
                         conv_w_ref, w_out_ref, g_post_mix_ref, g_pre_ffn_ref,
                         w_gate_ref, w_up_ref, w_down_ref, g_post_ffn_ref, o_ref):
    o_ref[...] = _mix_ffn_body(
        x_ref[...], attn_ref[...], gb_ref[...], u_prev_ref[...].astype(F32),
        u_ref[...].astype(F32), u_next_ref[...].astype(F32), conv_w_ref[...], w_out_ref,
        g_post_mix_ref[...], g_pre_ffn_ref[...], w_gate_ref, w_up_ref, w_down_ref,
        g_post_ffn_ref[...])


def _ffn_weight_specs(layer):
    return [
        _layer_spec((CONV_K, CONV_WIDTH), layer),
        _layer_spec((D_MODEL, D_MODEL), layer),
        _layer_spec((1, D_MODEL), layer),
        _layer_spec((1, D_MODEL), layer),
        _layer_spec((D_MODEL, D_FF), layer),
        _layer_spec((D_MODEL, D_FF), layer),
        _layer_spec((D_FF, D_MODEL), layer),
        _layer_spec((1, D_MODEL), layer),
    ]


def _ffn_weights(lw):
    return (lw["conv_w"], lw["w_out"], lw["g_post_mix"], lw["g_pre_ffn"], lw["w_gate"], lw["w_up"],
            lw["w_down"], lw["g_post_ffn"])


def _mix_ffn_real(x, attn, g_b, u, u_meta, lw, layer):
    b, seq, _ = x.shape
    nt = seq // ROW_TILE
    halo_per_tile = ROW_TILE // BF16_SUBLANES
    n_halo = seq // BF16_SUBLANES
    tile = lambda w: pl.BlockSpec((1, ROW_TILE, w), lambda i, j: (i, j, 0))
    in_specs = [
        tile(D_MODEL), tile(MLA_WIDTH), tile(CONV_WIDTH), tile(CONV_WIDTH),
        pl.BlockSpec((1, BF16_SUBLANES, CONV_WIDTH),
                     lambda i, j: (i, jnp.maximum(j * halo_per_tile - 1, 0), 0)),
        pl.BlockSpec((1, BF16_SUBLANES, CONV_WIDTH),
                     lambda i, j: (i, jnp.minimum((j + 1) * halo_per_tile, n_halo - 1), 0)),
        pl.BlockSpec((1, BF16_SUBLANES, CONV_WIDTH), lambda i, j: (i, 0, 0)),
    ] + _ffn_weight_specs(layer)
    return pl.pallas_call(
        _mix_ffn_kernel,
        grid=(b, nt),
        in_specs=in_specs,
        out_specs=tile(D_MODEL),
        out_shape=jax.ShapeDtypeStruct((b, seq, D_MODEL), F32),
        compiler_params=pltpu.CompilerParams(
            dimension_semantics=("parallel", "parallel"), vmem_limit_bytes=VMEM_LIMIT),
        name="mix_ffn_real",
    )(x, attn, g_b, u, u, u, u_meta, *_ffn_weights(lw))


def _mix_ffn_meta(xm, attn, g_b, u_prev, u, u_next, lw, layer):
    r = META_ROWS
    in_specs = [_const_spec((r, D_MODEL)), _const_spec((r, MLA_WIDTH))] \
        + [_const_spec((r, CONV_WIDTH)) for _ in range(4)] + _ffn_weight_specs(layer)
    return pl.pallas_call(
        _mix_ffn_meta_kernel,
        grid=(1,),
        in_specs=in_specs,
        out_specs=pl.BlockSpec((r, D_MODEL), lambda i: (0, 0)),
        out_shape=jax.ShapeDtypeStruct((r, D_MODEL), F32),
        compiler_params=pltpu.CompilerParams(
            dimension_semantics=("arbitrary",), vmem_limit_bytes=VMEM_LIMIT),
        name="mix_ffn_meta",
    )(xm, attn, g_b, u_prev, u, u_next, *_ffn_weights(lw))


def _pack_weights(pre_mix_g, w_in, q_norm_g, w_q_up, kv_norm_g, w_kv_up, conv_w, w_out,
                  post_mix_g, pre_ffn_g, w_gate, w_up, w_down, post_ffn_g):
    i2 = Q_LORA + KV_LORA + QK_ROPE
    zpad = jnp.zeros((DEPTH, D_MODEL, LANES - QK_ROPE), F32)
    w_in_p = jnp.concatenate([w_in[:, :, :i2], zpad, w_in[:, :, i2:]], axis=2).astype(BF16)
    wq_t = jnp.swapaxes(w_q_up, 1, 2).astype(BF16)
    wkv = w_kv_up.reshape(DEPTH, KV_LORA, HEADS, QK_NOPE + V_HEAD)
    wk = wkv[..., :QK_NOPE].reshape(DEPTH, KV_LORA, HEADS * QK_NOPE).astype(BF16)
    wv_t = jnp.swapaxes(wkv[..., QK_NOPE:].reshape(DEPTH, KV_LORA, HEADS * V_HEAD), 1, 2).astype(BF16)
    row = lambda g: g.reshape(DEPTH, 1, -1).astype(F32)
    return dict(
        g_pre=row(pre_mix_g), w_in=w_in_p, g_q=row(q_norm_g), wq_t=wq_t, g_kv=row(kv_norm_g),
        wk=wk, wv_t=wv_t, conv_w=conv_w.astype(F32), w_out=w_out.astype(BF16),
        g_post_mix=row(post_mix_g), g_pre_ffn=row(pre_ffn_g), w_gate=w_gate.astype(BF16),
        w_up=w_up.astype(BF16), w_down=w_down.astype(BF16), g_post_ffn=row(post_ffn_g))


def _rope_tables(pos):
    inv_freq = np.float32(ROPE_THETA) ** (-np.arange(0, QK_ROPE, 2, dtype=np.float32) / np.float32(QK_ROPE))
    ang = (pos.astype(np.float32)[:, None] * inv_freq.astype(np.float32)[None, :]).astype(np.float64)
    c, s = np.cos(ang).astype(np.float32), np.sin(ang).astype(np.float32)
    z = np.zeros((pos.shape[0], LANES - QK_ROPE), np.float32)
    cos = np.concatenate([c, c, z], axis=1)
    sin = np.concatenate([-s, s, z], axis=1)
    return tuple(jnp.asarray(t) for t in (cos, sin, np.ascontiguousarray(c.T), np.ascontiguousarray(s.T)))


def _query_chunk(seq):
    return 1024 if seq // K_CHUNK <= K_UNROLL else 512


def _meta_to_groups(a, batches, axis):
    out, r0 = [], 0
    for b in batches:
        sl = lax.slice_in_dim(a, r0, r0 + b * N_META, axis=axis)
        shape = a.shape[:axis] + (b, N_META) + a.shape[axis + 1:]
        out.append(sl.reshape(shape))
        r0 += b * N_META
    return out


def _join_meta(parts):
    flat = [p.reshape(-1, p.shape[-1]) for p in parts]
    rows = sum(f.shape[0] for f in flat)
    flat.append(jnp.zeros((META_ROWS - rows, flat[0].shape[-1]), flat[0].dtype))
    return jnp.concatenate(flat, axis=0)


def kernel(x_prompt, x_sample, meta_tokens, pre_mix_g, w_in, q_norm_g, w_q_up, kv_norm_g, w_kv_up,
           conv_w, w_out, post_mix_g, pre_ffn_g, w_gate, w_up, w_down, post_ffn_g):
    xs = [x_prompt, x_sample]
    batches = [x.shape[0] for x in xs]
    assert sum(batches) * N_META <= META_ROWS
    for x in xs:
        assert x.shape[1] % ROW_TILE == 0 and x.shape[2] == D_MODEL

    real_tabs = [_rope_tables(N_META + np.arange(x.shape[1])) for x in xs]
    meta_pos = np.arange(META_ROWS) % N_META
    meta_tabs = _rope_tables(meta_pos)

    xm = _join_meta([jnp.broadcast_to(meta_tokens.astype(F32)[None], (b, N_META, D_MODEL))
                     for b in batches])

    lw = _pack_weights(pre_mix_g, w_in, q_norm_g, w_q_up, kv_norm_g, w_kv_up, conv_w, w_out,
                       post_mix_g, pre_ffn_g, w_gate, w_up, w_down, post_ffn_g)
    for layer in range(DEPTH):
        qm, km, vm, gbm, um = _proj_meta(xm, lw, layer, meta_tabs)
        qm_g = _meta_to_groups(qm, batches, 2)
        km_g = _meta_to_groups(km, batches, 1)
        vm_g = _meta_to_groups(vm, batches, 2)
        um_g = _meta_to_groups(um, batches, 0)

        new_xs, attn_meta, u_prev_meta, u_next_meta = [], [], [], []
        for g, x in enumerate(xs):
            q_chunk = _query_chunk(x.shape[1])
            q_t, k, v_t, g_b, u = _proj_real(x, lw, layer, real_tabs[g], q_chunk)
            k_meta = jnp.pad(jnp.transpose(km_g[g], (1, 0, 2, 3)),
                             ((0, 0), (0, 0), (0, META_PAD - N_META), (0, 0)))
            v_meta_t = jnp.pad(jnp.transpose(vm_g[g], (2, 0, 1, 3)),
                               ((0, 0), (0, 0), (0, 0), (0, META_PAD - N_META)))
            q_meta_t = jnp.pad(jnp.transpose(qm_g[g], (2, 0, 1, 3)),
                               ((0, 0), (0, 0), (0, 0), (0, META_PAD - N_META)))[:, :, None]
            attn = _attention(q_t, k, v_t, k_meta, v_meta_t,
                              q_chunks_per_step=min(x.shape[1], Q_PER_STEP) // q_chunk)
            attn_m = _attention(q_meta_t, k, v_t, k_meta, v_meta_t, q_chunks_per_step=1)
            attn_meta.append(attn_m[:, :N_META])
            um_b = um_g[g]
            zero_row = jnp.zeros_like(um_b[:, :1])
            u_prev_meta.append(jnp.concatenate([zero_row, um_b[:, :-1]], axis=1))
            u_next_meta.append(jnp.concatenate([um_b[:, 1:], u[:, :1]], axis=1))
            new_xs.append(_mix_ffn_real(x, attn, g_b, u, um_b, lw, layer))

        xm = _mix_ffn_meta(xm, _join_meta(attn_meta), gbm, _join_meta(u_prev_meta), um,
                           _join_meta(u_next_meta), lw, layer)
        xs = new_xs

    return (xs[0], xs[1])
```

```python
import functools

import jax
import jax.numpy as jnp
import numpy as np
from jax import lax
from jax.experimental import pallas as pl
from jax.experimental.pallas import tpu as pltpu

F32 = jnp.float32
BF16 = jnp.bfloat16

D_MODEL = 1024
DEPTH = 2
N_META = 16
HEADS = 4
QK_NOPE = 128
QK_ROPE = 64
ROPE_HALF = QK_ROPE // 2
V_HEAD = 128
MLA_WIDTH = HEADS * V_HEAD
CONV_WIDTH = D_MODEL - MLA_WIDTH
CONV_K = 3
Q_LORA = 384
KV_LORA = 256
D_FF = 2816
ROPE_THETA = 10000.0
EPS = 1e-6
ATTN_SCALE = (QK_NOPE + QK_ROPE) ** -0.5
LOG2_E = 1.4426950408889634
Q_SCALE = ATTN_SCALE * LOG2_E

LANES = 128
BF16_SUBLANES = 16
VMEM_LIMIT = 56 * 1024 * 1024

QK_PAD = QK_NOPE + LANES
META_PAD = LANES
ROW_TILE = 512
FFN_TILE = 256
Q_PER_STEP = 4096
K_CHUNK = 512
K_UNROLL = 16
META_ROWS = 384
FF_CHUNKS = ((0, 1280), (1280, 2816))

C_Q0, C_KV0, C_KR0 = 0, Q_LORA, Q_LORA + KV_LORA
C_GB0 = C_KR0 + LANES
C_GC0 = C_GB0 + CONV_WIDTH
C_CH0 = C_GC0 + CONV_WIDTH
IN_COLS_P = C_CH0 + CONV_WIDTH
Q_HEAD_ROWS = QK_NOPE + QK_ROPE

NEG_BIG = -1e30


def _rms(x, g):
    ms = jnp.mean(x * x, axis=-1, keepdims=True)
    return x * lax.rsqrt(ms + EPS) * g


def _dot(a, b):
    return jnp.dot(a, b, preferred_element_type=F32)


def _dot_nt(a, b):
    return lax.dot_general(a, b, (((1,), (1,)), ((), ())), preferred_element_type=F32)


def _proj_body(x, g_pre, w_in_ref, g_q, wq_t_ref, g_kv, wk_ref, wv_t_ref, cos, sin, cos_t, sin_t):
    h = _rms(x, g_pre).astype(BF16)
    z = _dot(h, w_in_ref[...])
    cqn = _rms(z[:, C_Q0:C_KV0], g_q).astype(BF16)
    ckvn = _rms(z[:, C_KV0:C_KR0], g_kv).astype(BF16)
    k_r = z[:, C_KR0:C_GB0]
    lane = lax.broadcasted_iota(jnp.int32, k_r.shape, 1)
    partner = jnp.where(lane < ROPE_HALF, pltpu.roll(k_r, LANES - ROPE_HALF, 1),
                        pltpu.roll(k_r, ROPE_HALF, 1))
    k_rot = k_r * cos + partner * sin
    g_b = z[:, C_GB0:C_GC0]
    u = z[:, C_GC0:C_CH0] * z[:, C_CH0:IN_COLS_P]
    q_t = _dot_nt(wq_t_ref[...], cqn)
    k_nope = _dot(ckvn, wk_ref[...])
    v_t = _dot_nt(wv_t_ref[...], ckvn)
    q_heads = []
    for hd in range(HEADS):
        r0 = hd * Q_HEAD_ROWS
        nope = q_t[r0:r0 + QK_NOPE]
        x1 = q_t[r0 + QK_NOPE:r0 + QK_NOPE + ROPE_HALF]
        x2 = q_t[r0 + QK_NOPE + ROPE_HALF:r0 + Q_HEAD_ROWS]
        rot = jnp.concatenate([x1 * cos_t - x2 * sin_t, x2 * cos_t + x1 * sin_t,
                               jnp.zeros((QK_PAD - Q_HEAD_ROWS, q_t.shape[1]), F32)], axis=0)
        q_heads.append(((nope * Q_SCALE).astype(BF16), (rot * Q_SCALE).astype(BF16)))
    return q_heads, k_nope.astype(BF16), k_rot.astype(BF16), v_t.astype(BF16), g_b.astype(BF16), u.astype(BF16)


def _proj_kernel(x_ref, g_pre_ref, w_in_ref, g_q_ref, wq_t_ref, g_kv_ref, wk_ref, wv_t_ref,
                 cos_ref, sin_ref, cos_t_ref, sin_t_ref,
                 q_ref, k_ref, v_ref, gb_ref, u_ref):
    q_heads, k_nope, k_rot, v_t, g_b, u = _proj_body(
        x_ref[0], g_pre_ref[...], w_in_ref, g_q_ref[...], wq_t_ref, g_kv_ref[...], wk_ref,
        wv_t_ref, cos_ref[...], sin_ref[...], cos_t_ref[...], sin_t_ref[...])
    n_k = ROW_TILE // K_CHUNK
    for hd in range(HEADS):
        nope, rot = q_heads[hd]
        q_ref[0, hd, 0, :QK_NOPE, :] = nope
        q_ref[0, hd, 0, QK_NOPE:, :] = rot
        k_ref[0, hd, :, :QK_NOPE] = k_nope[:, hd * QK_NOPE:(hd + 1) * QK_NOPE]
        k_ref[0, hd, :, QK_NOPE:] = k_rot
        for c in range(n_k):
            v_ref[0, hd, c] = v_t[hd * V_HEAD:(hd + 1) * V_HEAD, c * K_CHUNK:(c + 1) * K_CHUNK]
    gb_ref[0] = g_b
    u_ref[0] = u


def _proj_meta_kernel(x_ref, g_pre_ref, w_in_ref, g_q_ref, wq_t_ref, g_kv_ref, wk_ref, wv_t_ref,
                      cos_ref, sin_ref, cos_t_ref, sin_t_ref,
                      q_ref, k_ref, v_ref, gb_ref, u_ref):
    q_heads, k_nope, k_rot, v_t, g_b, u = _proj_body(
        x_ref[...], g_pre_ref[...], w_in_ref, g_q_ref[...], wq_t_ref, g_kv_ref[...], wk_ref,
        wv_t_ref, cos_ref[...], sin_ref[...], cos_t_ref[...], sin_t_ref[...])
    for hd in range(HEADS):
        nope, rot = q_heads[hd]
        q_ref[hd, :QK_NOPE, :] = nope
        q_ref[hd, QK_NOPE:, :] = rot
        k_ref[hd, :, :QK_NOPE] = k_nope[:, hd * QK_NOPE:(hd + 1) * QK_NOPE]
        k_ref[hd, :, QK_NOPE:] = k_rot
        v_ref[hd] = v_t[hd * V_HEAD:(hd + 1) * V_HEAD, :]
    gb_ref[...] = g_b
    u_ref[...] = u


def _const_spec(shape):
    zeros = (0,) * len(shape)
    return pl.BlockSpec(shape, lambda *_: zeros, pipeline_mode=pl.Buffered(1))


def _layer_spec(shape, layer):
    zeros = (0,) * len(shape)
    return pl.BlockSpec((None,) + shape, lambda *_: (layer,) + zeros, pipeline_mode=pl.Buffered(1))


def _proj_weight_specs(layer):
    return [
        _layer_spec((1, D_MODEL), layer),
        _layer_spec((D_MODEL, IN_COLS_P), layer),
        _layer_spec((1, Q_LORA), layer),
        _layer_spec((HEADS * Q_HEAD_ROWS, Q_LORA), layer),
        _layer_spec((1, KV_LORA), layer),
        _layer_spec((KV_LORA, HEADS * QK_NOPE), layer),
        _layer_spec((HEADS * V_HEAD, KV_LORA), layer),
    ]


def _proj_real(x, lw, layer, tabs, q_chunk):
    b, seq, _ = x.shape
    nt = seq // ROW_TILE
    cos, sin, cos_t, sin_t = tabs
    in_specs = ([pl.BlockSpec((1, ROW_TILE, D_MODEL), lambda i, j: (i, j, 0))]
                + _proj_weight_specs(layer)
                + [pl.BlockSpec((ROW_TILE, LANES), lambda i, j: (j, 0)),
                   pl.BlockSpec((ROW_TILE, LANES), lambda i, j: (j, 0)),
                   pl.BlockSpec((ROPE_HALF, ROW_TILE), lambda i, j: (0, j)),
                   pl.BlockSpec((ROPE_HALF, ROW_TILE), lambda i, j: (0, j))])
    tiles_per_q = q_chunk // ROW_TILE
    n_k = ROW_TILE // K_CHUNK
    out_shape = [
        jax.ShapeDtypeStruct((b, HEADS, seq // q_chunk, QK_PAD, q_chunk), BF16),
        jax.ShapeDtypeStruct((b, HEADS, seq, QK_PAD), BF16),
        jax.ShapeDtypeStruct((b, HEADS, seq // K_CHUNK, V_HEAD, K_CHUNK), BF16),
        jax.ShapeDtypeStruct((b, seq, CONV_WIDTH), BF16),
        jax.ShapeDtypeStruct((b, seq, CONV_WIDTH), BF16),
    ]
    out_specs = [
        pl.BlockSpec((1, HEADS, 1, QK_PAD, ROW_TILE),
                     lambda i, j: (i, 0, j // tiles_per_q, 0, j % tiles_per_q)),
        pl.BlockSpec((1, HEADS, ROW_TILE, QK_PAD), lambda i, j: (i, 0, j, 0)),
        pl.BlockSpec((1, HEADS, n_k, V_HEAD, K_CHUNK), lambda i, j: (i, 0, j, 0, 0)),
        pl.BlockSpec((1, ROW_TILE, CONV_WIDTH), lambda i, j: (i, j, 0)),
        pl.BlockSpec((1, ROW_TILE, CONV_WIDTH), lambda i, j: (i, j, 0)),
    ]
    return pl.pallas_call(
        _proj_kernel,
        grid=(b, nt),
        in_specs=in_specs,
        out_specs=out_specs,
        out_shape=out_shape,
        compiler_params=pltpu.CompilerParams(
            dimension_semantics=("parallel", "parallel"), vmem_limit_bytes=VMEM_LIMIT),
        name="proj_real",
    )(x, lw["g_pre"], lw["w_in"], lw["g_q"], lw["wq_t"], lw["g_kv"], lw["wk"], lw["wv_t"],
      cos, sin, cos_t, sin_t)


def _proj_meta(xm, lw, layer, tabs):
    cos, sin, cos_t, sin_t = tabs
    r = META_ROWS
    in_specs = ([_const_spec((r, D_MODEL))] + _proj_weight_specs(layer)
                + [_const_spec((r, LANES)), _const_spec((r, LANES)),
                   _const_spec((ROPE_HALF, r)), _const_spec((ROPE_HALF, r))])
    out_shape = [
        jax.ShapeDtypeStruct((HEADS, QK_PAD, r), BF16),
        jax.ShapeDtypeStruct((HEADS, r, QK_PAD), BF16),
        jax.ShapeDtypeStruct((HEADS, V_HEAD, r), BF16),
        jax.ShapeDtypeStruct((r, CONV_WIDTH), BF16),
        jax.ShapeDtypeStruct((r, CONV_WIDTH), BF16),
    ]
    out_specs = [
        pl.BlockSpec((HEADS, QK_PAD, r), lambda i: (0, 0, 0)),
        pl.BlockSpec((HEADS, r, QK_PAD), lambda i: (0, 0, 0)),
        pl.BlockSpec((HEADS, V_HEAD, r), lambda i: (0, 0, 0)),
        pl.BlockSpec((r, CONV_WIDTH), lambda i: (0, 0)),
        pl.BlockSpec((r, CONV_WIDTH), lambda i: (0, 0)),
    ]
    return pl.pallas_call(
        _proj_meta_kernel,
        grid=(1,),
        in_specs=in_specs,
        out_specs=out_specs,
        out_shape=out_shape,
        compiler_params=pltpu.CompilerParams(
            dimension_semantics=("arbitrary",), vmem_limit_bytes=VMEM_LIMIT),
        name="proj_meta",
    )(xm, lw["g_pre"], lw["w_in"], lw["g_q"], lw["wq_t"], lw["g_kv"], lw["wk"], lw["wv_t"],
      cos, sin, cos_t, sin_t)


def _attn_kernel(q_ref, k_ref, v_ref, km_ref, vm_ref, o_ref, s_ref,
                 *, n_qc, n_kc, q_chunk, unroll):
    meta_valid = lax.broadcasted_iota(jnp.int32, (META_PAD, q_chunk), 0) < N_META
    n_groups = n_kc // unroll

    def ones_rows(n):
        first = lax.broadcasted_iota(jnp.int32, (BF16_SUBLANES, n), 0) == 0
        return jnp.where(first, 1.0, 0.0).astype(BF16)

    def put_scores(slot, kc, qc):
        k = k_ref[0, 0, pl.ds(pl.multiple_of(kc * K_CHUNK, K_CHUNK), K_CHUNK), :]
        s = _dot(k, q_ref[0, 0, qc])
        s_ref[slot] = s
        return jnp.max(s, axis=0, keepdims=True)

    def with_ones(v_t):
        return jnp.concatenate([v_t, ones_rows(v_t.shape[1])], axis=0)

    def accumulate(s, s_max, v_t, m, acc):
        m_new = jnp.maximum(m, s_max)
        p = jnp.exp2(s - m_new).astype(BF16)
        return m_new, jnp.exp2(m - m_new) * acc + _dot(with_ones(v_t), p)

    def k_group(base, qc, q_next, carry, last):
        m, acc, s_max = carry
        for t in range(unroll):
            final = last and t == unroll - 1
            if not final:
                next_max = put_scores((t + 1) % 2, base + t + 1, qc)
            elif q_next is not None:
                next_max = put_scores(0, 0, q_next)
            else:
                next_max = s_max
            m, acc = accumulate(s_ref[t % 2], s_max, v_ref[0, 0, base + t], m, acc)
            if last and t == 0:
                s_meta = jnp.where(meta_valid, _dot(km_ref[0, 0], q_ref[0, 0, qc]), NEG_BIG)
                m, acc = accumulate(s_meta, jnp.max(s_meta, axis=0, keepdims=True), vm_ref[0, 0], m, acc)
            s_max = next_max
        return m, acc, s_max

    def q_chunk_body(qc, q_next, s_max):
        m0 = jnp.full((1, q_chunk), NEG_BIG, F32)
        acc0 = jnp.zeros((V_HEAD + BF16_SUBLANES, q_chunk), F32)
        carry = (m0, acc0, s_max)
        if n_groups > 1:
            carry = lax.fori_loop(
                0, n_groups - 1, lambda g, c: k_group(g * unroll, qc, q_next, c, False), carry)
        _, acc, s_max = k_group((n_groups - 1) * unroll, qc, q_next, carry, True)
        out = (acc[:V_HEAD] / acc[V_HEAD:V_HEAD + 1]).T
        o_ref[0, pl.ds(pl.multiple_of(qc * q_chunk, q_chunk), q_chunk), :] = out.astype(o_ref.dtype)
        return s_max

    s_max = put_scores(0, 0, 0)
    if n_qc == 1:
        q_chunk_body(0, None, s_max)
    else:
        lax.fori_loop(
            0, n_qc, lambda qc, s_max: q_chunk_body(qc, jnp.minimum(qc + 1, n_qc - 1), s_max), s_max)


def _attention(q_t, k, v_t, k_meta, v_meta_t, *, q_chunks_per_step):
    b, _, n_q_total, _, q_chunk = q_t.shape
    seq_k = k.shape[2]
    n_kc = seq_k // K_CHUNK
    unroll = min(K_UNROLL, n_kc)
    assert unroll % 2 == 0 and n_kc % unroll == 0
    n_qc = q_chunks_per_step
    n_steps = n_q_total // n_qc
    tq = n_qc * q_chunk
    kern = functools.partial(_attn_kernel, n_qc=n_qc, n_kc=n_kc, q_chunk=q_chunk, unroll=unroll)
    return pl.pallas_call(
        kern,
        grid=(b, HEADS, n_steps),
        in_specs=[
            pl.BlockSpec((1, 1, n_qc, QK_PAD, q_chunk), lambda i, h, j: (i, h, j, 0, 0)),
            pl.BlockSpec((1, 1, seq_k, QK_PAD), lambda i, h, j: (i, h, 0, 0)),
            pl.BlockSpec((1, 1, n_kc, V_HEAD, K_CHUNK), lambda i, h, j: (i, h, 0, 0, 0)),
            pl.BlockSpec((1, 1, META_PAD, QK_PAD), lambda i, h, j: (i, h, 0, 0)),
            pl.BlockSpec((1, 1, V_HEAD, META_PAD), lambda i, h, j: (i, h, 0, 0)),
        ],
        out_specs=pl.BlockSpec((1, tq, V_HEAD), lambda i, h, j: (i, j, h)),
        out_shape=jax.ShapeDtypeStruct((b, n_q_total * q_chunk, MLA_WIDTH), BF16),
        scratch_shapes=[pltpu.VMEM((2, K_CHUNK, q_chunk), F32)],
        compiler_params=pltpu.CompilerParams(
            dimension_semantics=("parallel", "parallel", "parallel"), vmem_limit_bytes=VMEM_LIMIT),
        name="attention",
    )(q_t, k, v_t, k_meta, v_meta_t)


def _conv_body(g_b, u_prev, u_cur, u_next, conv_w):
    y = u_prev * conv_w[0:1] + u_cur * conv_w[1:2] + u_next * conv_w[2:3]
    return (g_b.astype(F32) * y).astype(BF16)


def _mix_body(x, attn, conv, w_out_ref, g_post_mix, g_pre_ffn):
    mix = _dot(attn, w_out_ref[:MLA_WIDTH, :]) + _dot(conv, w_out_ref[MLA_WIDTH:, :])
    x1 = x + _rms(mix, g_post_mix)
    return x1, _rms(x1, g_pre_ffn).astype(BF16)


def _ffn_chunk(h, cols, w_gate_ref, w_up_ref, w_down_ref):
    c0, c1 = cols
    gate = _dot(h, w_gate_ref[:, c0:c1])
    up = _dot(h, w_up_ref[:, c0:c1])
    act = (gate / (1.0 + jnp.exp(-gate)) * up).astype(BF16)
    return _dot(act, w_down_ref[c0:c1, :])


def _mix_ffn_body(x, attn, g_b, u_prev, u_cur, u_next, conv_w, w_out_ref, g_post_mix, g_pre_ffn,
                  w_gate_ref, w_up_ref, w_down_ref, g_post_ffn):
    conv = _conv_body(g_b, u_prev, u_cur, u_next, conv_w)
    x1, h = _mix_body(x, attn, conv, w_out_ref, g_post_mix, g_pre_ffn)
    first_cols, second_cols = FF_CHUNKS
    f = (_ffn_chunk(h, first_cols, w_gate_ref, w_up_ref, w_down_ref)
         + _ffn_chunk(h, second_cols, w_gate_ref, w_up_ref, w_down_ref))
    return x1 + _rms(f, g_post_ffn)


def _conv_neighbours(u_ref, u_before_ref, u_after_ref, u_meta_ref, j, tiles_per_seq):
    u_cur = u_ref[0].astype(F32)
    tail = BF16_SUBLANES - 1
    before = jnp.where(j == 0, u_meta_ref[0, tail:tail + 1, :], u_before_ref[0, tail:tail + 1, :])
    after = jnp.where(j == tiles_per_seq - 1, jnp.zeros_like(u_after_ref[0, 0:1, :]),
                      u_after_ref[0, 0:1, :])
    row = lax.broadcasted_iota(jnp.int32, u_cur.shape, 0)
    u_prev = jnp.where(row == 0, before.astype(F32), pltpu.roll(u_cur, 1, 0))
    u_next = jnp.where(row == FFN_TILE - 1, after.astype(F32), pltpu.roll(u_cur, FFN_TILE - 1, 0))
    return u_prev, u_cur, u_next


def _mix_ffn_kernel(*refs, tiles_per_seq, n_tiles):
    n_in = 7
    first, tile_b, tile_a = refs[0:n_in], refs[n_in:2 * n_in], refs[2 * n_in:3 * n_in]
    (conv_w_ref, w_out_ref, g_post_mix_ref, g_pre_ffn_ref, w_gate_ref, w_up_ref, w_down_ref,
     g_post_ffn_ref) = refs[3 * n_in:3 * n_in + 8]
    o_ref, x1_a, h_a, x1_b, h_b = refs[3 * n_in + 8:]
    s = pl.program_id(0)
    first_cols, second_cols = FF_CHUNKS

    def conv(tile, j):
        _, _, gb_ref, u_ref, u_before_ref, u_after_ref, u_meta_ref = tile
        u_prev, u_cur, u_next = _conv_neighbours(u_ref, u_before_ref, u_after_ref, u_meta_ref, j,
                                                 tiles_per_seq)
        return _conv_body(gb_ref[0], u_prev, u_cur, u_next, conv_w_ref[...])

    def mix(tile, conv_out, x1_scr, h_scr):
        x1, h = _mix_body(tile[0][0], tile[1][0], conv_out, w_out_ref, g_post_mix_ref[...],
                          g_pre_ffn_ref[...])
        x1_scr[...] = x1
        h_scr[...] = h

    def ffn_chunk(h_scr, cols):
        return _ffn_chunk(h_scr[...], cols, w_gate_ref, w_up_ref, w_down_ref)

    @pl.when(s == 0)
    def _():
        mix(first, conv(first, 0), x1_a, h_a)

    conv_b = conv(tile_b, lax.rem(2 * s + 1, tiles_per_seq))
    f_a = ffn_chunk(h_a, first_cols)
    mix(tile_b, conv_b, x1_b, h_b)
    f_a = f_a + ffn_chunk(h_a, second_cols)
    o_ref[0, :FFN_TILE, :] = x1_a[...] + _rms(f_a, g_post_ffn_ref[...])

    conv_a = conv(tile_a, lax.rem(jnp.minimum(2 * s + 2, n_tiles - 1), tiles_per_seq))
    f_b = ffn_chunk(h_b, first_cols)
    mix(tile_a, conv_a, x1_a, h_a)
    f_b = f_b + ffn_chunk(h_b, second_cols)
    o_ref[0, FFN_TILE:, :] = x1_b[...] + _rms(f_b, g_post_ffn_ref[...])


def _mix_ffn_meta_kernel(x_ref, attn_ref, gb_ref, u_prev_ref, u_ref, u_next_ref,
                         conv_w_ref, w_out_ref, g_post_mix_ref, g_pre_ffn_ref,
                         w_gate_ref, w_up_ref, w_down_ref, g_post_ffn_ref, o_ref):
    o_ref[...] = _mix_ffn_body(
        x_ref[...], attn_ref[...], gb_ref[...], u_prev_ref[...].astype(F32),
        u_ref[...].astype(F32), u_next_ref[...].astype(F32), conv_w_ref[...], w_out_ref,
        g_post_mix_ref[...], g_pre_ffn_ref[...], w_gate_ref, w_up_ref, w_down_ref,
        g_post_ffn_ref[...])


def _ffn_weight_specs(layer):
    return [
        _layer_spec((CONV_K, CONV_WIDTH), layer),
        _layer_spec((D_MODEL, D_MODEL), layer),
        _layer_spec((1, D_MODEL), layer),
        _layer_spec((1, D_MODEL), layer),
        _layer_spec((D_MODEL, D_FF), layer),
        _layer_spec((D_MODEL, D_FF), layer),
        _layer_spec((D_FF, D_MODEL), layer),
        _layer_spec((1, D_MODEL), layer),
    ]


def _ffn_weights(lw):
    return (lw["conv_w"], lw["w_out"], lw["g_post_mix"], lw["g_pre_ffn"], lw["w_gate"], lw["w_up"],
            lw["w_down"], lw["g_post_ffn"])


def _mix_ffn_real(x, attn, g_b, u, u_meta, lw, layer):
    b, seq, _ = x.shape
    tiles_per_seq = seq // FFN_TILE
    n_tiles = b * tiles_per_seq
    halo_per_tile = FFN_TILE // BF16_SUBLANES
    n_halo = seq // BF16_SUBLANES

    def tile_specs(tile_of_step, **kw):
        def at(fn):
            def index_map(s):
                q = tile_of_step(s)
                return fn(q // tiles_per_seq, q % tiles_per_seq)
            return index_map
        rows = lambda w: pl.BlockSpec((1, FFN_TILE, w), at(lambda i, j: (i, j, 0)), **kw)
        halo = lambda fn: pl.BlockSpec((1, BF16_SUBLANES, CONV_WIDTH), at(fn), **kw)
        return [rows(D_MODEL), rows(MLA_WIDTH), rows(CONV_WIDTH), rows(CONV_WIDTH),
                halo(lambda i, j: (i, jnp.maximum(j * halo_per_tile - 1, 0), 0)),
                halo(lambda i, j: (i, jnp.minimum((j + 1) * halo_per_tile, n_halo - 1), 0)),
                halo(lambda i, j: (i, 0, 0))]

    in_specs = (tile_specs(lambda s: 0 * s, pipeline_mode=pl.Buffered(1))
                + tile_specs(lambda s: 2 * s + 1)
                + tile_specs(lambda s: jnp.minimum(2 * s + 2, n_tiles - 1))
                + _ffn_weight_specs(layer))
    steps_per_seq = tiles_per_seq // 2
    kern = functools.partial(_mix_ffn_kernel, tiles_per_seq=tiles_per_seq, n_tiles=n_tiles)
    acts = (x, attn, g_b, u, u, u, u_meta)
    return pl.pallas_call(
        kern,
        grid=(n_tiles // 2,),
        in_specs=in_specs,
        out_specs=pl.BlockSpec((1, 2 * FFN_TILE, D_MODEL),
                               lambda s: (s // steps_per_seq, s % steps_per_seq, 0)),
        out_shape=jax.ShapeDtypeStruct((b, seq, D_MODEL), F32),
        scratch_shapes=[pltpu.VMEM((FFN_TILE, D_MODEL), F32), pltpu.VMEM((FFN_TILE, D_MODEL), BF16),
                        pltpu.VMEM((FFN_TILE, D_MODEL), F32), pltpu.VMEM((FFN_TILE, D_MODEL), BF16)],
        compiler_params=pltpu.CompilerParams(
            dimension_semantics=("arbitrary",), vmem_limit_bytes=VMEM_LIMIT),
        name="mix_ffn_real",
    )(*acts, *acts, *acts, *_ffn_weights(lw))


def _mix_ffn_meta(xm, attn, g_b, u_prev, u, u_next, lw, layer):
    r = META_ROWS
    in_specs = [_const_spec((r, D_MODEL)), _const_spec((r, MLA_WIDTH))] \
        + [_const_spec((r, CONV_WIDTH)) for _ in range(4)] + _ffn_weight_specs(layer)
    return pl.pallas_call(
        _mix_ffn_meta_kernel,
        grid=(1,),
        in_specs=in_specs,
        out_specs=pl.BlockSpec((r, D_MODEL), lambda i: (0, 0)),
        out_shape=jax.ShapeDtypeStruct((r, D_MODEL), F32),
        compiler_params=pltpu.CompilerParams(
            dimension_semantics=("arbitrary",), vmem_limit_bytes=VMEM_LIMIT),
        name="mix_ffn_meta",
    )(xm, attn, g_b, u_prev, u, u_next, *_ffn_weights(lw))


def _pack_weights(pre_mix_g, w_in, q_norm_g, w_q_up, kv_norm_g, w_kv_up, conv_w, w_out,
                  post_mix_g, pre_ffn_g, w_gate, w_up, w_down, post_ffn_g):
    i2 = Q_LORA + KV_LORA + QK_ROPE
    zpad = jnp.zeros((DEPTH, D_MODEL, LANES - QK_ROPE), F32)
    w_in_p = jnp.concatenate([w_in[:, :, :i2], zpad, w_in[:, :, i2:]], axis=2).astype(BF16)
    wq_t = jnp.swapaxes(w_q_up, 1, 2).astype(BF16)
    wkv = w_kv_up.reshape(DEPTH, KV_LORA, HEADS, QK_NOPE + V_HEAD)
    wk = wkv[..., :QK_NOPE].reshape(DEPTH, KV_LORA, HEADS * QK_NOPE).astype(BF16)
    wv_t = jnp.swapaxes(wkv[..., QK_NOPE:].reshape(DEPTH, KV_LORA, HEADS * V_HEAD), 1, 2).astype(BF16)
    row = lambda g: g.reshape(DEPTH, 1, -1).astype(F32)
    return dict(
        g_pre=row(pre_mix_g), w_in=w_in_p, g_q=row(q_norm_g), wq_t=wq_t, g_kv=row(kv_norm_g),
        wk=wk, wv_t=wv_t, conv_w=conv_w.astype(F32), w_out=w_out.astype(BF16),
        g_post_mix=row(post_mix_g), g_pre_ffn=row(pre_ffn_g), w_gate=w_gate.astype(BF16),
        w_up=w_up.astype(BF16), w_down=w_down.astype(BF16), g_post_ffn=row(post_ffn_g))


def _rope_tables(pos):
    inv_freq = np.float32(ROPE_THETA) ** (-np.arange(0, QK_ROPE, 2, dtype=np.float32) / np.float32(QK_ROPE))
    ang = (pos.astype(np.float32)[:, None] * inv_freq.astype(np.float32)[None, :]).astype(np.float64)
    c, s = np.cos(ang).astype(np.float32), np.sin(ang).astype(np.float32)
    z = np.zeros((pos.shape[0], LANES - QK_ROPE), np.float32)
    cos = np.concatenate([c, c, z], axis=1)
    sin = np.concatenate([-s, s, z], axis=1)
    return tuple(jnp.asarray(t) for t in (cos, sin, np.ascontiguousarray(c.T), np.ascontiguousarray(s.T)))


def _query_chunk(seq):
    return 1024 if seq // K_CHUNK <= K_UNROLL else 512


def _meta_to_groups(a, batches, axis):
    out, r0 = [], 0
    for b in batches:
        sl = lax.slice_in_dim(a, r0, r0 + b * N_META, axis=axis)
        shape = a.shape[:axis] + (b, N_META) + a.shape[axis + 1:]
        out.append(sl.reshape(shape))
        r0 += b * N_META
    return out


def _join_meta(parts):
    flat = [p.reshape(-1, p.shape[-1]) for p in parts]
    rows = sum(f.shape[0] for f in flat)
    flat.append(jnp.zeros((META_ROWS - rows, flat[0].shape[-1]), flat[0].dtype))
    return jnp.concatenate(flat, axis=0)


def kernel(x_prompt, x_sample, meta_tokens, pre_mix_g, w_in, q_norm_g, w_q_up, kv_norm_g, w_kv_up,
           conv_w, w_out, post_mix_g, pre_ffn_g, w_gate, w_up, w_down, post_ffn_g):
    xs = [x_prompt, x_sample]
    batches = [x.shape[0] for x in xs]
    assert sum(batches) * N_META <= META_ROWS
    for x in xs:
        assert x.shape[1] % max(ROW_TILE, 2 * FFN_TILE) == 0 and x.shape[2] == D_MODEL

    real_tabs = [_rope_tables(N_META + np.arange(x.shape[1])) for x in xs]
    meta_pos = np.arange(META_ROWS) % N_META
    meta_tabs = _rope_tables(meta_pos)

    xm = _join_meta([jnp.broadcast_to(meta_tokens.astype(F32)[None], (b, N_META, D_MODEL))
                     for b in batches])

    lw = _pack_weights(pre_mix_g, w_in, q_norm_g, w_q_up, kv_norm_g, w_kv_up, conv_w, w_out,
                       post_mix_g, pre_ffn_g, w_gate, w_up, w_down, post_ffn_g)
    for layer in range(DEPTH):
        qm, km, vm, gbm, um = _proj_meta(xm, lw, layer, meta_tabs)
        qm_g = _meta_to_groups(qm, batches, 2)
        km_g = _meta_to_groups(km, batches, 1)
        vm_g = _meta_to_groups(vm, batches, 2)
        um_g = _meta_to_groups(um, batches, 0)

        new_xs, attn_meta, u_prev_meta, u_next_meta = [], [], [], []
        for g, x in enumerate(xs):
            q_chunk = _query_chunk(x.shape[1])
            q_t, k, v_t, g_b, u = _proj_real(x, lw, layer, real_tabs[g], q_chunk)
            k_meta = jnp.pad(jnp.transpose(km_g[g], (1, 0, 2, 3)),
                             ((0, 0), (0, 0), (0, META_PAD - N_META), (0, 0)))
            v_meta_t = jnp.pad(jnp.transpose(vm_g[g], (2, 0, 1, 3)),
                               ((0, 0), (0, 0), (0, 0), (0, META_PAD - N_META)))
            q_meta_t = jnp.pad(jnp.transpose(qm_g[g], (2, 0, 1, 3)),
                               ((0, 0), (0, 0), (0, 0), (0, META_PAD - N_META)))[:, :, None]
            attn = _attention(q_t, k, v_t, k_meta, v_meta_t,
                              q_chunks_per_step=min(x.shape[1], Q_PER_STEP) // q_chunk)
            attn_m = _attention(q_meta_t, k, v_t, k_meta, v_meta_t, q_chunks_per_step=1)
            attn_meta.append(attn_m[:, :N_META])
            um_b = um_g[g]
            zero_row = jnp.zeros_like(um_b[:, :1])
            u_prev_meta.append(jnp.concatenate([zero_row, um_b[:, :-1]], axis=1))
            u_next_meta.append(jnp.concatenate([um_b[:, 1:], u[:, :1]], axis=1))
            new_xs.append(_mix_ffn_real(x, attn, g_b, u, um_b, lw, layer))

        xm = _mix_ffn_meta(xm, _join_meta(attn_meta), gbm, _join_meta(u_prev_meta), um,
                           _join_meta(u_next_meta), lw, layer)
        xs = new_xs

    return (xs[0], xs[1])
```

```python
import functools

import jax
import jax.numpy as jnp
import numpy as np
from jax import lax
from jax.experimental import pallas as pl
from jax.experimental.pallas import tpu as pltpu

F32 = jnp.float32
BF16 = jnp.bfloat16

D_MODEL = 1024
DEPTH = 2
N_META = 16
HEADS = 4
QK_NOPE = 128
QK_ROPE = 64
ROPE_HALF = QK_ROPE // 2
V_HEAD = 128
MLA_WIDTH = HEADS * V_HEAD
CONV_WIDTH = D_MODEL - MLA_WIDTH
CONV_K = 3
Q_LORA = 384
KV_LORA = 256
D_FF = 2816
ROPE_THETA = 10000.0
EPS = 1e-6
ATTN_SCALE = (QK_NOPE + QK_ROPE) ** -0.5
LOG2_E = 1.4426950408889634
Q_SCALE = ATTN_SCALE * LOG2_E

LANES = 128
BF16_SUBLANES = 16
VMEM_LIMIT = 56 * 1024 * 1024

QK_PAD = QK_NOPE + LANES
META_PAD = LANES
ROW_TILE = 512
FFN_TILE = 256
Q_PER_STEP = 4096
K_CHUNK = 512
K_UNROLL = 16
META_ROWS = 384
FF_CHUNKS = ((0, 1280), (1280, 2816))

C_Q0, C_KV0, C_KR0 = 0, Q_LORA, Q_LORA + KV_LORA
C_GB0 = C_KR0 + LANES
C_GC0 = C_GB0 + CONV_WIDTH
C_CH0 = C_GC0 + CONV_WIDTH
IN_COLS_P = C_CH0 + CONV_WIDTH
Q_HEAD_ROWS = QK_NOPE + QK_ROPE

NEG_BIG = -1e30


def _rms(x, g):
    ms = jnp.mean(x * x, axis=-1, keepdims=True)
    return x * lax.rsqrt(ms + EPS) * g


def _dot(a, b):
    return jnp.dot(a, b, preferred_element_type=F32)


def _dot_nt(a, b):
    return lax.dot_general(a, b, (((1,), (1,)), ((), ())), preferred_element_type=F32)


def _proj_body(x, g_pre, w_in_ref, g_q, wq_t_ref, g_kv, wk_ref, wv_t_ref, cos, sin, cos_t, sin_t):
    h = _rms(x, g_pre).astype(BF16)
    z = _dot(h, w_in_ref[...])
    cqn = _rms(z[:, C_Q0:C_KV0], g_q).astype(BF16)
    ckvn = _rms(z[:, C_KV0:C_KR0], g_kv).astype(BF16)
    k_r = z[:, C_KR0:C_GB0]
    lane = lax.broadcasted_iota(jnp.int32, k_r.shape, 1)
    partner = jnp.where(lane < ROPE_HALF, pltpu.roll(k_r, LANES - ROPE_HALF, 1),
                        pltpu.roll(k_r, ROPE_HALF, 1))
    k_rot = k_r * cos + partner * sin
    g_b = z[:, C_GB0:C_GC0]
    u = z[:, C_GC0:C_CH0] * z[:, C_CH0:IN_COLS_P]
    q_t = _dot_nt(wq_t_ref[...], cqn)
    k_nope = _dot(ckvn, wk_ref[...])
    v_t = _dot_nt(wv_t_ref[...], ckvn)
    q_heads = []
    for hd in range(HEADS):
        r0 = hd * Q_HEAD_ROWS
        nope = q_t[r0:r0 + QK_NOPE]
        x1 = q_t[r0 + QK_NOPE:r0 + QK_NOPE + ROPE_HALF]
        x2 = q_t[r0 + QK_NOPE + ROPE_HALF:r0 + Q_HEAD_ROWS]
        rot = jnp.concatenate([x1 * cos_t - x2 * sin_t, x2 * cos_t + x1 * sin_t,
                               jnp.zeros((QK_PAD - Q_HEAD_ROWS, q_t.shape[1]), F32)], axis=0)
        q_heads.append(((nope * Q_SCALE).astype(BF16), (rot * Q_SCALE).astype(BF16)))
    return q_heads, k_nope.astype(BF16), k_rot.astype(BF16), v_t.astype(BF16), g_b.astype(BF16), u.astype(BF16)


def _proj_kernel(x_ref, g_pre_ref, w_in_ref, g_q_ref, wq_t_ref, g_kv_ref, wk_ref, wv_t_ref,
                 cos_ref, sin_ref, cos_t_ref, sin_t_ref,
                 q_ref, k_ref, v_ref, gb_ref, u_ref):
    q_heads, k_nope, k_rot, v_t, g_b, u = _proj_body(
        x_ref[0], g_pre_ref[...], w_in_ref, g_q_ref[...], wq_t_ref, g_kv_ref[...], wk_ref,
        wv_t_ref, cos_ref[...], sin_ref[...], cos_t_ref[...], sin_t_ref[...])
    n_k = ROW_TILE // K_CHUNK
    for hd in range(HEADS):
        nope, rot = q_heads[hd]
        q_ref[0, hd, 0, :QK_NOPE, :] = nope
        q_ref[0, hd, 0, QK_NOPE:, :] = rot
        k_ref[0, hd, :, :QK_NOPE] = k_nope[:, hd * QK_NOPE:(hd + 1) * QK_NOPE]
        k_ref[0, hd, :, QK_NOPE:] = k_rot
        for c in range(n_k):
            v_ref[0, hd, c] = v_t[hd * V_HEAD:(hd + 1) * V_HEAD, c * K_CHUNK:(c + 1) * K_CHUNK]
    gb_ref[0] = g_b
    u_ref[0] = u


def _proj_meta_kernel(x_ref, g_pre_ref, w_in_ref, g_q_ref, wq_t_ref, g_kv_ref, wk_ref, wv_t_ref,
                      cos_ref, sin_ref, cos_t_ref, sin_t_ref,
                      q_ref, k_ref, v_ref, gb_ref, u_ref):
    q_heads, k_nope, k_rot, v_t, g_b, u = _proj_body(
        x_ref[...], g_pre_ref[...], w_in_ref, g_q_ref[...], wq_t_ref, g_kv_ref[...], wk_ref,
        wv_t_ref, cos_ref[...], sin_ref[...], cos_t_ref[...], sin_t_ref[...])
    for hd in range(HEADS):
        nope, rot = q_heads[hd]
        q_ref[hd, :QK_NOPE, :] = nope
        q_ref[hd, QK_NOPE:, :] = rot
        k_ref[hd, :, :QK_NOPE] = k_nope[:, hd * QK_NOPE:(hd + 1) * QK_NOPE]
        k_ref[hd, :, QK_NOPE:] = k_rot
        v_ref[hd] = v_t[hd * V_HEAD:(hd + 1) * V_HEAD, :]
    gb_ref[...] = g_b
    u_ref[...] = u


def _const_spec(shape):
    zeros = (0,) * len(shape)
    return pl.BlockSpec(shape, lambda *_: zeros, pipeline_mode=pl.Buffered(1))


def _layer_spec(shape, layer):
    zeros = (0,) * len(shape)
    return pl.BlockSpec((None,) + shape, lambda *_: (layer,) + zeros, pipeline_mode=pl.Buffered(1))


def _proj_weight_specs(layer):
    return [
        _layer_spec((1, D_MODEL), layer),
        _layer_spec((D_MODEL, IN_COLS_P), layer),
        _layer_spec((1, Q_LORA), layer),
        _layer_spec((HEADS * Q_HEAD_ROWS, Q_LORA), layer),
        _layer_spec((1, KV_LORA), layer),
        _layer_spec((KV_LORA, HEADS * QK_NOPE), layer),
        _layer_spec((HEADS * V_HEAD, KV_LORA), layer),
    ]


def _proj_real(x, lw, layer, tabs, q_chunk):
    b, seq, _ = x.shape
    nt = seq // ROW_TILE
    cos, sin, cos_t, sin_t = tabs
    in_specs = ([pl.BlockSpec((1, ROW_TILE, D_MODEL), lambda i, j: (i, j, 0))]
                + _proj_weight_specs(layer)
                + [pl.BlockSpec((ROW_TILE, LANES), lambda i, j: (j, 0)),
                   pl.BlockSpec((ROW_TILE, LANES), lambda i, j: (j, 0)),
                   pl.BlockSpec((ROPE_HALF, ROW_TILE), lambda i, j: (0, j)),
                   pl.BlockSpec((ROPE_HALF, ROW_TILE), lambda i, j: (0, j))])
    tiles_per_q = q_chunk // ROW_TILE
    n_k = ROW_TILE // K_CHUNK
    out_shape = [
        jax.ShapeDtypeStruct((b, HEADS, seq // q_chunk, QK_PAD, q_chunk), BF16),
        jax.ShapeDtypeStruct((b, HEADS, seq, QK_PAD), BF16),
        jax.ShapeDtypeStruct((b, HEADS, seq // K_CHUNK, V_HEAD, K_CHUNK), BF16),
        jax.ShapeDtypeStruct((b, seq, CONV_WIDTH), BF16),
        jax.ShapeDtypeStruct((b, seq, CONV_WIDTH), BF16),
    ]
    out_specs = [
        pl.BlockSpec((1, HEADS, 1, QK_PAD, ROW_TILE),
                     lambda i, j: (i, 0, j // tiles_per_q, 0, j % tiles_per_q)),
        pl.BlockSpec((1, HEADS, ROW_TILE, QK_PAD), lambda i, j: (i, 0, j, 0)),
        pl.BlockSpec((1, HEADS, n_k, V_HEAD, K_CHUNK), lambda i, j: (i, 0, j, 0, 0)),
        pl.BlockSpec((1, ROW_TILE, CONV_WIDTH), lambda i, j: (i, j, 0)),
        pl.BlockSpec((1, ROW_TILE, CONV_WIDTH), lambda i, j: (i, j, 0)),
    ]
    return pl.pallas_call(
        _proj_kernel,
        grid=(b, nt),
        in_specs=in_specs,
        out_specs=out_specs,
        out_shape=out_shape,
        compiler_params=pltpu.CompilerParams(
            dimension_semantics=("parallel", "parallel"), vmem_limit_bytes=VMEM_LIMIT),
        name="proj_real",
    )(x, lw["g_pre"], lw["w_in"], lw["g_q"], lw["wq_t"], lw["g_kv"], lw["wk"], lw["wv_t"],
      cos, sin, cos_t, sin_t)


def _proj_meta(xm, lw, layer, tabs):
    cos, sin, cos_t, sin_t = tabs
    r = META_ROWS
    in_specs = ([_const_spec((r, D_MODEL))] + _proj_weight_specs(layer)
                + [_const_spec((r, LANES)), _const_spec((r, LANES)),
                   _const_spec((ROPE_HALF, r)), _const_spec((ROPE_HALF, r))])
    out_shape = [
        jax.ShapeDtypeStruct((HEADS, QK_PAD, r), BF16),
        jax.ShapeDtypeStruct((HEADS, r, QK_PAD), BF16),
        jax.ShapeDtypeStruct((HEADS, V_HEAD, r), BF16),
        jax.ShapeDtypeStruct((r, CONV_WIDTH), BF16),
        jax.ShapeDtypeStruct((r, CONV_WIDTH), BF16),
    ]
    out_specs = [
        pl.BlockSpec((HEADS, QK_PAD, r), lambda i: (0, 0, 0)),
        pl.BlockSpec((HEADS, r, QK_PAD), lambda i: (0, 0, 0)),
        pl.BlockSpec((HEADS, V_HEAD, r), lambda i: (0, 0, 0)),
        pl.BlockSpec((r, CONV_WIDTH), lambda i: (0, 0)),
        pl.BlockSpec((r, CONV_WIDTH), lambda i: (0, 0)),
    ]
    return pl.pallas_call(
        _proj_meta_kernel,
        grid=(1,),
        in_specs=in_specs,
        out_specs=out_specs,
        out_shape=out_shape,
        compiler_params=pltpu.CompilerParams(
            dimension_semantics=("arbitrary",), vmem_limit_bytes=VMEM_LIMIT),
        name="proj_meta",
    )(xm, lw["g_pre"], lw["w_in"], lw["g_q"], lw["wq_t"], lw["g_kv"], lw["wk"], lw["wv_t"],
      cos, sin, cos_t, sin_t)


def _attn_kernel(q_ref, k_ref, v_ref, km_ref, vm_ref, o_ref, s_ref, sm_ref, acc_ref,
                 *, n_qc, n_kc, q_chunk, unroll):
    meta_valid = lax.broadcasted_iota(jnp.int32, (META_PAD, q_chunk), 0) < N_META
    n_groups = n_kc // unroll

    def ones_rows(n):
        first = lax.broadcasted_iota(jnp.int32, (BF16_SUBLANES, n), 0) == 0
        return jnp.where(first, 1.0, 0.0).astype(BF16)

    def put_scores(slot, kc, qc):
        k = k_ref[0, 0, pl.ds(pl.multiple_of(kc * K_CHUNK, K_CHUNK), K_CHUNK), :]
        s = _dot(k, q_ref[0, 0, qc])
        s_ref[slot] = s
        return jnp.max(s, axis=0, keepdims=True)

    def with_ones(v_t):
        return jnp.concatenate([v_t, ones_rows(v_t.shape[1])], axis=0)

    def accumulate(s, s_max, v_t, m, acc):
        m_new = jnp.maximum(m, s_max)
        p = jnp.exp2(s - m_new).astype(BF16)
        return m_new, jnp.exp2(m - m_new) * acc + _dot(with_ones(v_t), p)

    def finalize(acc, qc):
        out = (acc[:V_HEAD] / acc[V_HEAD:V_HEAD + 1]).T
        o_ref[0, pl.ds(pl.multiple_of(qc * q_chunk, q_chunk), q_chunk), :] = out.astype(o_ref.dtype)

    def k_group(base, qc, q_next, carry, first, last):
        m, acc, s_max = carry
        for t in range(unroll):
            final = last and t == unroll - 1
            if not final:
                next_max = put_scores((t + 1) % 2, base + t + 1, qc)
            elif q_next is not None:
                next_max = put_scores(0, 0, q_next)
            else:
                next_max = s_max
            if first and t == 0 and n_qc > 1:
                finalize(acc_ref[...], jnp.maximum(qc - 1, 0))
            if last and t == 0:
                s_meta = jnp.where(meta_valid, _dot(km_ref[0, 0], q_ref[0, 0, qc]), NEG_BIG)
                sm_ref[...] = s_meta
                meta_max = jnp.max(s_meta, axis=0, keepdims=True)
            m, acc = accumulate(s_ref[t % 2], s_max, v_ref[0, 0, base + t], m, acc)
            if last and t == 1:
                m, acc = accumulate(sm_ref[...], meta_max, vm_ref[0, 0], m, acc)
            s_max = next_max
        return m, acc, s_max

    def q_chunk_body(qc, q_next, s_max):
        m0 = jnp.full((1, q_chunk), NEG_BIG, F32)
        acc0 = jnp.zeros((V_HEAD + BF16_SUBLANES, q_chunk), F32)
        carry = (m0, acc0, s_max)
        if n_groups > 1:
            carry = k_group(0, qc, q_next, carry, True, False)
        if n_groups > 2:
            carry = lax.fori_loop(
                1, n_groups - 1, lambda g, c: k_group(g * unroll, qc, q_next, c, False, False), carry)
        _, acc, s_max = k_group((n_groups - 1) * unroll, qc, q_next, carry, n_groups == 1, True)
        if n_qc == 1:
            finalize(acc, qc)
        else:
            acc_ref[...] = acc
        return s_max

    s_max = put_scores(0, 0, 0)
    if n_qc == 1:
        q_chunk_body(0, None, s_max)
    else:
        acc_ref[...] = jnp.ones(acc_ref.shape, F32)
        lax.fori_loop(
            0, n_qc, lambda qc, s_max: q_chunk_body(qc, jnp.minimum(qc + 1, n_qc - 1), s_max), s_max)
        finalize(acc_ref[...], n_qc - 1)


def _attention(q_t, k, v_t, k_meta, v_meta_t, *, q_chunks_per_step):
    b, _, n_q_total, _, q_chunk = q_t.shape
    seq_k = k.shape[2]
    n_kc = seq_k // K_CHUNK
    unroll = min(K_UNROLL, n_kc)
    assert unroll % 2 == 0 and n_kc % unroll == 0
    n_qc = q_chunks_per_step
    n_steps = n_q_total // n_qc
    tq = n_qc * q_chunk
    kern = functools.partial(_attn_kernel, n_qc=n_qc, n_kc=n_kc, q_chunk=q_chunk, unroll=unroll)
    return pl.pallas_call(
        kern,
        grid=(b, HEADS, n_steps),
        in_specs=[
            pl.BlockSpec((1, 1, n_qc, QK_PAD, q_chunk), lambda i, h, j: (i, h, j, 0, 0)),
            pl.BlockSpec((1, 1, seq_k, QK_PAD), lambda i, h, j: (i, h, 0, 0)),
            pl.BlockSpec((1, 1, n_kc, V_HEAD, K_CHUNK), lambda i, h, j: (i, h, 0, 0, 0)),
            pl.BlockSpec((1, 1, META_PAD, QK_PAD), lambda i, h, j: (i, h, 0, 0)),
            pl.BlockSpec((1, 1, V_HEAD, META_PAD), lambda i, h, j: (i, h, 0, 0)),
        ],
        out_specs=pl.BlockSpec((1, tq, V_HEAD), lambda i, h, j: (i, j, h)),
        out_shape=jax.ShapeDtypeStruct((b, n_q_total * q_chunk, MLA_WIDTH), BF16),
        scratch_shapes=[pltpu.VMEM((2, K_CHUNK, q_chunk), F32), pltpu.VMEM((META_PAD, q_chunk), F32),
                        pltpu.VMEM((V_HEAD + BF16_SUBLANES, q_chunk), F32)],
        compiler_params=pltpu.CompilerParams(
            dimension_semantics=("parallel", "parallel", "parallel"), vmem_limit_bytes=VMEM_LIMIT),
        name="attention",
    )(q_t, k, v_t, k_meta, v_meta_t)


def _conv_body(g_b, u_prev, u_cur, u_next, conv_w):
    y = u_prev * conv_w[0:1] + u_cur * conv_w[1:2] + u_next * conv_w[2:3]
    return (g_b.astype(F32) * y).astype(BF16)


def _mix_body(x, attn, conv, w_out_ref, g_post_mix, g_pre_ffn):
    mix = _dot(attn, w_out_ref[:MLA_WIDTH, :]) + _dot(conv, w_out_ref[MLA_WIDTH:, :])
    x1 = x + _rms(mix, g_post_mix)
    return x1, _rms(x1, g_pre_ffn).astype(BF16)


def _ffn_chunk(h, cols, w_gate_ref, w_up_ref, w_down_ref):
    c0, c1 = cols
    gate = _dot(h, w_gate_ref[:, c0:c1])
    up = _dot(h, w_up_ref[:, c0:c1])
    act = (gate / (1.0 + jnp.exp(-gate)) * up).astype(BF16)
    return _dot(act, w_down_ref[c0:c1, :])


def _mix_ffn_body(x, attn, g_b, u_prev, u_cur, u_next, conv_w, w_out_ref, g_post_mix, g_pre_ffn,
                  w_gate_ref, w_up_ref, w_down_ref, g_post_ffn):
    conv = _conv_body(g_b, u_prev, u_cur, u_next, conv_w)
    x1, h = _mix_body(x, attn, conv, w_out_ref, g_post_mix, g_pre_ffn)
    first_cols, second_cols = FF_CHUNKS
    f = (_ffn_chunk(h, first_cols, w_gate_ref, w_up_ref, w_down_ref)
         + _ffn_chunk(h, second_cols, w_gate_ref, w_up_ref, w_down_ref))
    return x1 + _rms(f, g_post_ffn)


def _conv_neighbours(u_ref, u_before_ref, u_after_ref, u_meta_ref, j, tiles_per_seq):
    u_cur = u_ref[0].astype(F32)
    tail = BF16_SUBLANES - 1
    before = jnp.where(j == 0, u_meta_ref[0, tail:tail + 1, :], u_before_ref[0, tail:tail + 1, :])
    after = jnp.where(j == tiles_per_seq - 1, jnp.zeros_like(u_after_ref[0, 0:1, :]),
                      u_after_ref[0, 0:1, :])
    row = lax.broadcasted_iota(jnp.int32, u_cur.shape, 0)
    u_prev = jnp.where(row == 0, before.astype(F32), pltpu.roll(u_cur, 1, 0))
    u_next = jnp.where(row == FFN_TILE - 1, after.astype(F32), pltpu.roll(u_cur, FFN_TILE - 1, 0))
    return u_prev, u_cur, u_next


def _mix_ffn_kernel(*refs, tiles_per_seq, n_tiles):
    n_in = 7
    first, tile_b, tile_a = refs[0:n_in], refs[n_in:2 * n_in], refs[2 * n_in:3 * n_in]
    (conv_w_ref, w_out_ref, g_post_mix_ref, g_pre_ffn_ref, w_gate_ref, w_up_ref, w_down_ref,
     g_post_ffn_ref) = refs[3 * n_in:3 * n_in + 8]
    o_ref, x1_a, h_a, x1_b, h_b = refs[3 * n_in + 8:]
    s = pl.program_id(0)
    first_cols, second_cols = FF_CHUNKS

    def conv(tile, j):
        _, _, gb_ref, u_ref, u_before_ref, u_after_ref, u_meta_ref = tile
        u_prev, u_cur, u_next = _conv_neighbours(u_ref, u_before_ref, u_after_ref, u_meta_ref, j,
                                                 tiles_per_seq)
        return _conv_body(gb_ref[0], u_prev, u_cur, u_next, conv_w_ref[...])

    def mix(tile, conv_out, x1_scr, h_scr):
        x1, h = _mix_body(tile[0][0], tile[1][0], conv_out, w_out_ref, g_post_mix_ref[...],
                          g_pre_ffn_ref[...])
        x1_scr[...] = x1
        h_scr[...] = h

    def ffn_chunk(h_scr, cols):
        return _ffn_chunk(h_scr[...], cols, w_gate_ref, w_up_ref, w_down_ref)

    @pl.when(s == 0)
    def _():
        mix(first, conv(first, 0), x1_a, h_a)

    conv_b = conv(tile_b, lax.rem(2 * s + 1, tiles_per_seq))
    f_a = ffn_chunk(h_a, first_cols)
    mix(tile_b, conv_b, x1_b, h_b)
    f_a = f_a + ffn_chunk(h_a, second_cols)
    o_ref[0, :FFN_TILE, :] = x1_a[...] + _rms(f_a, g_post_ffn_ref[...])

    conv_a = conv(tile_a, lax.rem(jnp.minimum(2 * s + 2, n_tiles - 1), tiles_per_seq))
    f_b = ffn_chunk(h_b, first_cols)
    mix(tile_a, conv_a, x1_a, h_a)
    f_b = f_b + ffn_chunk(h_b, second_cols)
    o_ref[0, FFN_TILE:, :] = x1_b[...] + _rms(f_b, g_post_ffn_ref[...])


def _mix_ffn_meta_kernel(x_ref, attn_ref, gb_ref, u_prev_ref, u_ref, u_next_ref,
                         conv_w_ref, w_out_ref, g_post_mix_ref, g_pre_ffn_ref,
                         w_gate_ref, w_up_ref, w_down_ref, g_post_ffn_ref, o_ref):
    o_ref[...] = _mix_ffn_body(
        x_ref[...], attn_ref[...], gb_ref[...], u_prev_ref[...].astype(F32),
        u_ref[...].astype(F32), u_next_ref[...].astype(F32), conv_w_ref[...], w_out_ref,
        g_post_mix_ref[...], g_pre_ffn_ref[...], w_gate_ref, w_up_ref, w_down_ref,
        g_post_ffn_ref[...])


def _ffn_weight_specs(layer):
    return [
        _layer_spec((CONV_K, CONV_WIDTH), layer),
        _layer_spec((D_MODEL, D_MODEL), layer),
        _layer_spec((1, D_MODEL), layer),
        _layer_spec((1, D_MODEL), layer),
        _layer_spec((D_MODEL, D_FF), layer),
        _layer_spec((D_MODEL, D_FF), layer),
        _layer_spec((D_FF, D_MODEL), layer),
        _layer_spec((1, D_MODEL), layer),
    ]


def _ffn_weights(lw):
    return (lw["conv_w"], lw["w_out"], lw["g_post_mix"], lw["g_pre_ffn"], lw["w_gate"], lw["w_up"],
            lw["w_down"], lw["g_post_ffn"])


def _mix_ffn_real(x, attn, g_b, u, u_meta, lw, layer):
    b, seq, _ = x.shape
    tiles_per_seq = seq // FFN_TILE
    n_tiles = b * tiles_per_seq
    halo_per_tile = FFN_TILE // BF16_SUBLANES
    n_halo = seq // BF16_SUBLANES

    def tile_specs(tile_of_step, **kw):
        def at(fn):
            def index_map(s):
                q = tile_of_step(s)
                return fn(q // tiles_per_seq, q % tiles_per_seq)
            return index_map
        rows = lambda w: pl.BlockSpec((1, FFN_TILE, w), at(lambda i, j: (i, j, 0)), **kw)
        halo = lambda fn: pl.BlockSpec((1, BF16_SUBLANES, CONV_WIDTH), at(fn), **kw)
        return [rows(D_MODEL), rows(MLA_WIDTH), rows(CONV_WIDTH), rows(CONV_WIDTH),
                halo(lambda i, j: (i, jnp.maximum(j * halo_per_tile - 1, 0), 0)),
                halo(lambda i, j: (i, jnp.minimum((j + 1) * halo_per_tile, n_halo - 1), 0)),
                halo(lambda i, j: (i, 0, 0))]

    in_specs = (tile_specs(lambda s: 0 * s, pipeline_mode=pl.Buffered(1))
                + tile_specs(lambda s: 2 * s + 1)
                + tile_specs(lambda s: jnp.minimum(2 * s + 2, n_tiles - 1))
                + _ffn_weight_specs(layer))
    steps_per_seq = tiles_per_seq // 2
    kern = functools.partial(_mix_ffn_kernel, tiles_per_seq=tiles_per_seq, n_tiles=n_tiles)
    acts = (x, attn, g_b, u, u, u, u_meta)
    return pl.pallas_call(
        kern,
        grid=(n_tiles // 2,),
        in_specs=in_specs,
        out_specs=pl.BlockSpec((1, 2 * FFN_TILE, D_MODEL),
                               lambda s: (s // steps_per_seq, s % steps_per_seq, 0)),
        out_shape=jax.ShapeDtypeStruct((b, seq, D_MODEL), F32),
        scratch_shapes=[pltpu.VMEM((FFN_TILE, D_MODEL), F32), pltpu.VMEM((FFN_TILE, D_MODEL), BF16),
                        pltpu.VMEM((FFN_TILE, D_MODEL), F32), pltpu.VMEM((FFN_TILE, D_MODEL), BF16)],
        compiler_params=pltpu.CompilerParams(
            dimension_semantics=("arbitrary",), vmem_limit_bytes=VMEM_LIMIT),
        name="mix_ffn_real",
    )(*acts, *acts, *acts, *_ffn_weights(lw))


def _mix_ffn_meta(xm, attn, g_b, u_prev, u, u_next, lw, layer):
    r = META_ROWS
    in_specs = [_const_spec((r, D_MODEL)), _const_spec((r, MLA_WIDTH))] \
        + [_const_spec((r, CONV_WIDTH)) for _ in range(4)] + _ffn_weight_specs(layer)
    return pl.pallas_call(
        _mix_ffn_meta_kernel,
        grid=(1,),
        in_specs=in_specs,
        out_specs=pl.BlockSpec((r, D_MODEL), lambda i: (0, 0)),
        out_shape=jax.ShapeDtypeStruct((r, D_MODEL), F32),
        compiler_params=pltpu.CompilerParams(
            dimension_semantics=("arbitrary",), vmem_limit_bytes=VMEM_LIMIT),
        name="mix_ffn_meta",
    )(xm, attn, g_b, u_prev, u, u_next, *_ffn_weights(lw))


def _pack_weights(pre_mix_g, w_in, q_norm_g, w_q_up, kv_norm_g, w_kv_up, conv_w, w_out,
                  post_mix_g, pre_ffn_g, w_gate, w_up, w_down, post_ffn_g):
    i2 = Q_LORA + KV_LORA + QK_ROPE
    zpad = jnp.zeros((DEPTH, D_MODEL, LANES - QK_ROPE), F32)
    w_in_p = jnp.concatenate([w_in[:, :, :i2], zpad, w_in[:, :, i2:]], axis=2).astype(BF16)
    wq_t = jnp.swapaxes(w_q_up, 1, 2).astype(BF16)
    wkv = w_kv_up.reshape(DEPTH, KV_LORA, HEADS, QK_NOPE + V_HEAD)
    wk = wkv[..., :QK_NOPE].reshape(DEPTH, KV_LORA, HEADS * QK_NOPE).astype(BF16)
    wv_t = jnp.swapaxes(wkv[..., QK_NOPE:].reshape(DEPTH, KV_LORA, HEADS * V_HEAD), 1, 2).astype(BF16)
    row = lambda g: g.reshape(DEPTH, 1, -1).astype(F32)
    return dict(
        g_pre=row(pre_mix_g), w_in=w_in_p, g_q=row(q_norm_g), wq_t=wq_t, g_kv=row(kv_norm_g),
        wk=wk, wv_t=wv_t, conv_w=conv_w.astype(F32), w_out=w_out.astype(BF16),
        g_post_mix=row(post_mix_g), g_pre_ffn=row(pre_ffn_g), w_gate=w_gate.astype(BF16),
        w_up=w_up.astype(BF16), w_down=w_down.astype(BF16), g_post_ffn=row(post_ffn_g))


def _rope_tables(pos):
    inv_freq = np.float32(ROPE_THETA) ** (-np.arange(0, QK_ROPE, 2, dtype=np.float32) / np.float32(QK_ROPE))
    ang = (pos.astype(np.float32)[:, None] * inv_freq.astype(np.float32)[None, :]).astype(np.float64)
    c, s = np.cos(ang).astype(np.float32), np.sin(ang).astype(np.float32)
    z = np.zeros((pos.shape[0], LANES - QK_ROPE), np.float32)
    cos = np.concatenate([c, c, z], axis=1)
    sin = np.concatenate([-s, s, z], axis=1)
    return tuple(jnp.asarray(t) for t in (cos, sin, np.ascontiguousarray(c.T), np.ascontiguousarray(s.T)))


def _query_chunk(seq):
    return 1024 if seq // K_CHUNK <= K_UNROLL else 512


def _meta_to_groups(a, batches, axis):
    out, r0 = [], 0
    for b in batches:
        sl = lax.slice_in_dim(a, r0, r0 + b * N_META, axis=axis)
        shape = a.shape[:axis] + (b, N_META) + a.shape[axis + 1:]
        out.append(sl.reshape(shape))
        r0 += b * N_META
    return out


def _join_meta(parts):
    flat = [p.reshape(-1, p.shape[-1]) for p in parts]
    rows = sum(f.shape[0] for f in flat)
    flat.append(jnp.zeros((META_ROWS - rows, flat[0].shape[-1]), flat[0].dtype))
    return jnp.concatenate(flat, axis=0)


def kernel(x_prompt, x_sample, meta_tokens, pre_mix_g, w_in, q_norm_g, w_q_up, kv_norm_g, w_kv_up,
           conv_w, w_out, post_mix_g, pre_ffn_g, w_gate, w_up, w_down, post_ffn_g):
    xs = [x_prompt, x_sample]
    batches = [x.shape[0] for x in xs]
    assert sum(batches) * N_META <= META_ROWS
    for x in xs:
        assert x.shape[1] % max(ROW_TILE, 2 * FFN_TILE) == 0 and x.shape[2] == D_MODEL

    real_tabs = [_rope_tables(N_META + np.arange(x.shape[1])) for x in xs]
    meta_pos = np.arange(META_ROWS) % N_META
    meta_tabs = _rope_tables(meta_pos)

    xm = _join_meta([jnp.broadcast_to(meta_tokens.astype(F32)[None], (b, N_META, D_MODEL))
                     for b in batches])

    lw = _pack_weights(pre_mix_g, w_in, q_norm_g, w_q_up, kv_norm_g, w_kv_up, conv_w, w_out,
                       post_mix_g, pre_ffn_g, w_gate, w_up, w_down, post_ffn_g)
    for layer in range(DEPTH):
        qm, km, vm, gbm, um = _proj_meta(xm, lw, layer, meta_tabs)
        qm_g = _meta_to_groups(qm, batches, 2)
        km_g = _meta_to_groups(km, batches, 1)
        vm_g = _meta_to_groups(vm, batches, 2)
        um_g = _meta_to_groups(um, batches, 0)

        new_xs, attn_meta, u_prev_meta, u_next_meta = [], [], [], []
        for g, x in enumerate(xs):
            q_chunk = _query_chunk(x.shape[1])
            q_t, k, v_t, g_b, u = _proj_real(x, lw, layer, real_tabs[g], q_chunk)
            k_meta = jnp.pad(jnp.transpose(km_g[g], (1, 0, 2, 3)),
                             ((0, 0), (0, 0), (0, META_PAD - N_META), (0, 0)))
            v_meta_t = jnp.pad(jnp.transpose(vm_g[g], (2, 0, 1, 3)),
                               ((0, 0), (0, 0), (0, 0), (0, META_PAD - N_META)))
            q_meta_t = jnp.pad(jnp.transpose(qm_g[g], (2, 0, 1, 3)),
                               ((0, 0), (0, 0), (0, 0), (0, META_PAD - N_META)))[:, :, None]
            attn = _attention(q_t, k, v_t, k_meta, v_meta_t,
                              q_chunks_per_step=min(x.shape[1], Q_PER_STEP) // q_chunk)
            attn_m = _attention(q_meta_t, k, v_t, k_meta, v_meta_t, q_chunks_per_step=1)
            attn_meta.append(attn_m[:, :N_META])
            um_b = um_g[g]
            zero_row = jnp.zeros_like(um_b[:, :1])
            u_prev_meta.append(jnp.concatenate([zero_row, um_b[:, :-1]], axis=1))
            u_next_meta.append(jnp.concatenate([um_b[:, 1:], u[:, :1]], axis=1))
            new_xs.append(_mix_ffn_real(x, attn, g_b, u, um_b, lw, layer))

        xm = _mix_ffn_meta(xm, _join_meta(attn_meta), gbm, _join_meta(u_prev_meta), um,
                           _join_meta(u_next_meta), lw, layer)
        xs = new_xs

    return (xs[0], xs[1])
```

```python
import functools

import jax
import jax.numpy as jnp
import numpy as np
from jax import lax
from jax.experimental import pallas as pl
from jax.experimental.pallas import tpu as pltpu

F32 = jnp.float32
BF16 = jnp.bfloat16

D_MODEL = 1024
DEPTH = 2
N_META = 16
HEADS = 4
QK_NOPE = 128
QK_ROPE = 64
ROPE_HALF = QK_ROPE // 2
V_HEAD = 128
MLA_WIDTH = HEADS * V_HEAD
CONV_WIDTH = D_MODEL - MLA_WIDTH
CONV_K = 3
Q_LORA = 384
KV_LORA = 256
D_FF = 2816
ROPE_THETA = 10000.0
EPS = 1e-6
ATTN_SCALE = (QK_NOPE + QK_ROPE) ** -0.5
LOG2_E = 1.4426950408889634
Q_SCALE = ATTN_SCALE * LOG2_E

LANES = 128
BF16_SUBLANES = 16
VMEM_LIMIT = 56 * 1024 * 1024

QK_PAD = QK_NOPE + LANES
META_PAD = LANES
ROW_TILE = 512
FFN_TILE = 256
Q_PER_STEP = 8192
K_CHUNK = 512
K_UNROLL = 16
META_ROWS = 384
FF_CHUNKS = ((0, 1536), (1536, 2816))

C_Q0, C_KV0, C_KR0 = 0, Q_LORA, Q_LORA + KV_LORA
C_GB0 = C_KR0 + LANES
C_GC0 = C_GB0 + CONV_WIDTH
C_CH0 = C_GC0 + CONV_WIDTH
IN_COLS_P = C_CH0 + CONV_WIDTH
Q_HEAD_ROWS = QK_NOPE + QK_ROPE

NEG_BIG = -1e30


def _rms(x, g):
    ms = jnp.mean(x * x, axis=-1, keepdims=True)
    return x * lax.rsqrt(ms + EPS) * g


def _dot(a, b):
    return jnp.dot(a, b, preferred_element_type=F32)


def _dot_nt(a, b):
    return lax.dot_general(a, b, (((1,), (1,)), ((), ())), preferred_element_type=F32)


def _proj_body(x, g_pre, w_in_ref, g_q, wq_t_ref, g_kv, wk_ref, wv_t_ref, cos, sin, cos_t, sin_t):
    h = _rms(x, g_pre).astype(BF16)
    z = _dot(h, w_in_ref[...])
    cqn = _rms(z[:, C_Q0:C_KV0], g_q).astype(BF16)
    ckvn = _rms(z[:, C_KV0:C_KR0], g_kv).astype(BF16)
    k_r = z[:, C_KR0:C_GB0]
    lane = lax.broadcasted_iota(jnp.int32, k_r.shape, 1)
    partner = jnp.where(lane < ROPE_HALF, pltpu.roll(k_r, LANES - ROPE_HALF, 1),
                        pltpu.roll(k_r, ROPE_HALF, 1))
    k_rot = k_r * cos + partner * sin
    g_b = z[:, C_GB0:C_GC0]
    u = z[:, C_GC0:C_CH0] * z[:, C_CH0:IN_COLS_P]
    q_t = _dot_nt(wq_t_ref[...], cqn)
    k_nope = _dot(ckvn, wk_ref[...])
    v_t = _dot_nt(wv_t_ref[...], ckvn)
    q_heads = []
    for hd in range(HEADS):
        r0 = hd * Q_HEAD_ROWS
        nope = q_t[r0:r0 + QK_NOPE]
        x1 = q_t[r0 + QK_NOPE:r0 + QK_NOPE + ROPE_HALF]
        x2 = q_t[r0 + QK_NOPE + ROPE_HALF:r0 + Q_HEAD_ROWS]
        rot = jnp.concatenate([x1 * cos_t - x2 * sin_t, x2 * cos_t + x1 * sin_t,
                               jnp.zeros((QK_PAD - Q_HEAD_ROWS, q_t.shape[1]), F32)], axis=0)
        q_heads.append(((nope * Q_SCALE).astype(BF16), (rot * Q_SCALE).astype(BF16)))
    return q_heads, k_nope.astype(BF16), k_rot.astype(BF16), v_t.astype(BF16), g_b.astype(BF16), u.astype(BF16)


def _proj_kernel(x_ref, g_pre_ref, w_in_ref, g_q_ref, wq_t_ref, g_kv_ref, wk_ref, wv_t_ref,
                 cos_ref, sin_ref, cos_t_ref, sin_t_ref,
                 q_ref, k_ref, v_ref, gb_ref, u_ref):
    q_heads, k_nope, k_rot, v_t, g_b, u = _proj_body(
        x_ref[0], g_pre_ref[...], w_in_ref, g_q_ref[...], wq_t_ref, g_kv_ref[...], wk_ref,
        wv_t_ref, cos_ref[...], sin_ref[...], cos_t_ref[...], sin_t_ref[...])
    n_k = ROW_TILE // K_CHUNK
    for hd in range(HEADS):
        nope, rot = q_heads[hd]
        q_ref[0, hd, 0, :QK_NOPE, :] = nope
        q_ref[0, hd, 0, QK_NOPE:, :] = rot
        k_ref[0, hd, :, :QK_NOPE] = k_nope[:, hd * QK_NOPE:(hd + 1) * QK_NOPE]
        k_ref[0, hd, :, QK_NOPE:] = k_rot
        for c in range(n_k):
            v_ref[0, hd, c] = v_t[hd * V_HEAD:(hd + 1) * V_HEAD, c * K_CHUNK:(c + 1) * K_CHUNK]
    gb_ref[0] = g_b
    u_ref[0] = u


def _proj_meta_kernel(x_ref, g_pre_ref, w_in_ref, g_q_ref, wq_t_ref, g_kv_ref, wk_ref, wv_t_ref,
                      cos_ref, sin_ref, cos_t_ref, sin_t_ref,
                      q_ref, k_ref, v_ref, gb_ref, u_ref):
    q_heads, k_nope, k_rot, v_t, g_b, u = _proj_body(
        x_ref[...], g_pre_ref[...], w_in_ref, g_q_ref[...], wq_t_ref, g_kv_ref[...], wk_ref,
        wv_t_ref, cos_ref[...], sin_ref[...], cos_t_ref[...], sin_t_ref[...])
    for hd in range(HEADS):
        nope, rot = q_heads[hd]
        q_ref[hd, :QK_NOPE, :] = nope
        q_ref[hd, QK_NOPE:, :] = rot
        k_ref[hd, :, :QK_NOPE] = k_nope[:, hd * QK_NOPE:(hd + 1) * QK_NOPE]
        k_ref[hd, :, QK_NOPE:] = k_rot
        v_ref[hd] = v_t[hd * V_HEAD:(hd + 1) * V_HEAD, :]
    gb_ref[...] = g_b
    u_ref[...] = u


def _const_spec(shape):
    zeros = (0,) * len(shape)
    return pl.BlockSpec(shape, lambda *_: zeros, pipeline_mode=pl.Buffered(1))


def _layer_spec(shape, layer):
    zeros = (0,) * len(shape)
    return pl.BlockSpec((None,) + shape, lambda *_: (layer,) + zeros, pipeline_mode=pl.Buffered(1))


def _proj_weight_specs(layer):
    return [
        _layer_spec((1, D_MODEL), layer),
        _layer_spec((D_MODEL, IN_COLS_P), layer),
        _layer_spec((1, Q_LORA), layer),
        _layer_spec((HEADS * Q_HEAD_ROWS, Q_LORA), layer),
        _layer_spec((1, KV_LORA), layer),
        _layer_spec((KV_LORA, HEADS * QK_NOPE), layer),
        _layer_spec((HEADS * V_HEAD, KV_LORA), layer),
    ]


def _proj_real(x, lw, layer, tabs, q_chunk):
    b, seq, _ = x.shape
    nt = seq // ROW_TILE
    cos, sin, cos_t, sin_t = tabs
    in_specs = ([pl.BlockSpec((1, ROW_TILE, D_MODEL), lambda i, j: (i, j, 0))]
                + _proj_weight_specs(layer)
                + [pl.BlockSpec((ROW_TILE, LANES), lambda i, j: (j, 0)),
                   pl.BlockSpec((ROW_TILE, LANES), lambda i, j: (j, 0)),
                   pl.BlockSpec((ROPE_HALF, ROW_TILE), lambda i, j: (0, j)),
                   pl.BlockSpec((ROPE_HALF, ROW_TILE), lambda i, j: (0, j))])
    tiles_per_q = q_chunk // ROW_TILE
    n_k = ROW_TILE // K_CHUNK
    out_shape = [
        jax.ShapeDtypeStruct((b, HEADS, seq // q_chunk, QK_PAD, q_chunk), BF16),
        jax.ShapeDtypeStruct((b, HEADS, seq, QK_PAD), BF16),
        jax.ShapeDtypeStruct((b, HEADS, seq // K_CHUNK, V_HEAD, K_CHUNK), BF16),
        jax.ShapeDtypeStruct((b, seq, CONV_WIDTH), BF16),
        jax.ShapeDtypeStruct((b, seq, CONV_WIDTH), BF16),
    ]
    out_specs = [
        pl.BlockSpec((1, HEADS, 1, QK_PAD, ROW_TILE),
                     lambda i, j: (i, 0, j // tiles_per_q, 0, j % tiles_per_q)),
        pl.BlockSpec((1, HEADS, ROW_TILE, QK_PAD), lambda i, j: (i, 0, j, 0)),
        pl.BlockSpec((1, HEADS, n_k, V_HEAD, K_CHUNK), lambda i, j: (i, 0, j, 0, 0)),
        pl.BlockSpec((1, ROW_TILE, CONV_WIDTH), lambda i, j: (i, j, 0)),
        pl.BlockSpec((1, ROW_TILE, CONV_WIDTH), lambda i, j: (i, j, 0)),
    ]
    return pl.pallas_call(
        _proj_kernel,
        grid=(b, nt),
        in_specs=in_specs,
        out_specs=out_specs,
        out_shape=out_shape,
        compiler_params=pltpu.CompilerParams(
            dimension_semantics=("parallel", "parallel"), vmem_limit_bytes=VMEM_LIMIT),
        name="proj_real",
    )(x, lw["g_pre"], lw["w_in"], lw["g_q"], lw["wq_t"], lw["g_kv"], lw["wk"], lw["wv_t"],
      cos, sin, cos_t, sin_t)


def _proj_meta(xm, lw, layer, tabs):
    cos, sin, cos_t, sin_t = tabs
    r = META_ROWS
    in_specs = ([_const_spec((r, D_MODEL))] + _proj_weight_specs(layer)
                + [_const_spec((r, LANES)), _const_spec((r, LANES)),
                   _const_spec((ROPE_HALF, r)), _const_spec((ROPE_HALF, r))])
    out_shape = [
        jax.ShapeDtypeStruct((HEADS, QK_PAD, r), BF16),
        jax.ShapeDtypeStruct((HEADS, r, QK_PAD), BF16),
        jax.ShapeDtypeStruct((HEADS, V_HEAD, r), BF16),
        jax.ShapeDtypeStruct((r, CONV_WIDTH), BF16),
        jax.ShapeDtypeStruct((r, CONV_WIDTH), BF16),
    ]
    out_specs = [
        pl.BlockSpec((HEADS, QK_PAD, r), lambda i: (0, 0, 0)),
        pl.BlockSpec((HEADS, r, QK_PAD), lambda i: (0, 0, 0)),
        pl.BlockSpec((HEADS, V_HEAD, r), lambda i: (0, 0, 0)),
        pl.BlockSpec((r, CONV_WIDTH), lambda i: (0, 0)),
        pl.BlockSpec((r, CONV_WIDTH), lambda i: (0, 0)),
    ]
    return pl.pallas_call(
        _proj_meta_kernel,
        grid=(1,),
        in_specs=in_specs,
        out_specs=out_specs,
        out_shape=out_shape,
        compiler_params=pltpu.CompilerParams(
            dimension_semantics=("arbitrary",), vmem_limit_bytes=VMEM_LIMIT),
        name="proj_meta",
    )(xm, lw["g_pre"], lw["w_in"], lw["g_q"], lw["wq_t"], lw["g_kv"], lw["wk"], lw["wv_t"],
      cos, sin, cos_t, sin_t)


def _attn_kernel(q_ref, k_ref, v_ref, km_ref, vm_ref, o_ref, s_ref, sm_ref, acc_ref,
                 *, n_qc, n_kc, q_chunk, unroll):
    meta_valid = lax.broadcasted_iota(jnp.int32, (META_PAD, q_chunk), 0) < N_META
    n_groups = n_kc // unroll

    def ones_rows(n):
        first = lax.broadcasted_iota(jnp.int32, (BF16_SUBLANES, n), 0) == 0
        return jnp.where(first, 1.0, 0.0).astype(BF16)

    def put_scores(slot, kc, qc):
        k = k_ref[0, 0, pl.ds(pl.multiple_of(kc * K_CHUNK, K_CHUNK), K_CHUNK), :]
        s = _dot(k, q_ref[0, 0, qc])
        s_ref[slot] = s
        return jnp.max(s, axis=0, keepdims=True)

    def with_ones(v_t):
        return jnp.concatenate([v_t, ones_rows(v_t.shape[1])], axis=0)

    def accumulate(s, s_max, v_t, m, acc):
        m_new = jnp.maximum(m, s_max)
        p = jnp.exp2(s - m_new).astype(BF16)
        return m_new, jnp.exp2(m - m_new) * acc + _dot(with_ones(v_t), p)

    def finalize(acc, qc):
        out = (acc[:V_HEAD] / acc[V_HEAD:V_HEAD + 1]).T
        o_ref[0, pl.ds(pl.multiple_of(qc * q_chunk, q_chunk), q_chunk), :] = out.astype(o_ref.dtype)

    def k_group(base, qc, q_next, carry, first, last):
        m, acc, s_max = carry
        for t in range(unroll):
            final = last and t == unroll - 1
            if not final:
                next_max = put_scores((t + 1) % 2, base + t + 1, qc)
            elif q_next is not None:
                next_max = put_scores(0, 0, q_next)
            else:
                next_max = s_max
            if first and t == 0 and n_qc > 1:
                finalize(acc_ref[...], jnp.maximum(qc - 1, 0))
            if last and t == 0:
                s_meta = jnp.where(meta_valid, _dot(km_ref[0, 0], q_ref[0, 0, qc]), NEG_BIG)
                sm_ref[...] = s_meta
                meta_max = jnp.max(s_meta, axis=0, keepdims=True)
            m, acc = accumulate(s_ref[t % 2], s_max, v_ref[0, 0, base + t], m, acc)
            if last and t == 1:
                m, acc = accumulate(sm_ref[...], meta_max, vm_ref[0, 0], m, acc)
            s_max = next_max
        return m, acc, s_max

    def q_chunk_body(qc, q_next, s_max):
        m0 = jnp.full((1, q_chunk), NEG_BIG, F32)
        acc0 = jnp.zeros((V_HEAD + BF16_SUBLANES, q_chunk), F32)
        carry = (m0, acc0, s_max)
        if n_groups > 1:
            carry = k_group(0, qc, q_next, carry, True, False)
        if n_groups > 2:
            carry = lax.fori_loop(
                1, n_groups - 1, lambda g, c: k_group(g * unroll, qc, q_next, c, False, False), carry)
        _, acc, s_max = k_group((n_groups - 1) * unroll, qc, q_next, carry, n_groups == 1, True)
        if n_qc == 1:
            finalize(acc, qc)
        else:
            acc_ref[...] = acc
        return s_max

    s_max = put_scores(0, 0, 0)
    if n_qc == 1:
        q_chunk_body(0, None, s_max)
    else:
        acc_ref[...] = jnp.ones(acc_ref.shape, F32)
        chunks_per_trip = 2 if n_qc % 2 == 0 and n_qc > 2 else 1

        def q_trip(i, s_max):
            for c in range(chunks_per_trip):
                qc = chunks_per_trip * i + c
                s_max = q_chunk_body(qc, jnp.minimum(qc + 1, n_qc - 1), s_max)
            return s_max

        lax.fori_loop(0, n_qc // chunks_per_trip, q_trip, s_max)
        finalize(acc_ref[...], n_qc - 1)


def _attention(q_t, k, v_t, k_meta, v_meta_t, *, q_chunks_per_step):
    b, _, n_q_total, _, q_chunk = q_t.shape
    seq_k = k.shape[2]
    n_kc = seq_k // K_CHUNK
    unroll = min(K_UNROLL, n_kc)
    assert unroll % 2 == 0 and n_kc % unroll == 0
    n_qc = q_chunks_per_step
    n_steps = n_q_total // n_qc
    tq = n_qc * q_chunk
    kern = functools.partial(_attn_kernel, n_qc=n_qc, n_kc=n_kc, q_chunk=q_chunk, unroll=unroll)
    return pl.pallas_call(
        kern,
        grid=(b, HEADS, n_steps),
        in_specs=[
            pl.BlockSpec((1, 1, n_qc, QK_PAD, q_chunk), lambda i, h, j: (i, h, j, 0, 0)),
            pl.BlockSpec((1, 1, seq_k, QK_PAD), lambda i, h, j: (i, h, 0, 0)),
            pl.BlockSpec((1, 1, n_kc, V_HEAD, K_CHUNK), lambda i, h, j: (i, h, 0, 0, 0)),
            pl.BlockSpec((1, 1, META_PAD, QK_PAD), lambda i, h, j: (i, h, 0, 0)),
            pl.BlockSpec((1, 1, V_HEAD, META_PAD), lambda i, h, j: (i, h, 0, 0)),
        ],
        out_specs=pl.BlockSpec((1, tq, V_HEAD), lambda i, h, j: (i, j, h)),
        out_shape=jax.ShapeDtypeStruct((b, n_q_total * q_chunk, MLA_WIDTH), BF16),
        scratch_shapes=[pltpu.VMEM((2, K_CHUNK, q_chunk), F32), pltpu.VMEM((META_PAD, q_chunk), F32),
                        pltpu.VMEM((V_HEAD + BF16_SUBLANES, q_chunk), F32)],
        compiler_params=pltpu.CompilerParams(
            dimension_semantics=("parallel", "parallel", "parallel"), vmem_limit_bytes=VMEM_LIMIT),
        name="attention",
    )(q_t, k, v_t, k_meta, v_meta_t)


def _conv_body(g_b, u_prev, u_cur, u_next, conv_w):
    y = u_prev * conv_w[0:1] + u_cur * conv_w[1:2] + u_next * conv_w[2:3]
    return (g_b.astype(F32) * y).astype(BF16)


def _mix_body(x, attn, conv, w_out_ref, g_post_mix, g_pre_ffn):
    mix = _dot(attn, w_out_ref[:MLA_WIDTH, :]) + _dot(conv, w_out_ref[MLA_WIDTH:, :])
    x1 = x + _rms(mix, g_post_mix)
    return x1, _rms(x1, g_pre_ffn).astype(BF16)


def _ffn_chunk(h, cols, w_gate_ref, w_up_ref, w_down_ref):
    c0, c1 = cols
    gate = _dot(h, w_gate_ref[:, c0:c1])
    up = _dot(h, w_up_ref[:, c0:c1])
    act = (gate / (1.0 + jnp.exp(-gate)) * up).astype(BF16)
    return _dot(act, w_down_ref[c0:c1, :])


def _mix_ffn_body(x, attn, g_b, u_prev, u_cur, u_next, conv_w, w_out_ref, g_post_mix, g_pre_ffn,
                  w_gate_ref, w_up_ref, w_down_ref, g_post_ffn):
    conv = _conv_body(g_b, u_prev, u_cur, u_next, conv_w)
    x1, h = _mix_body(x, attn, conv, w_out_ref, g_post_mix, g_pre_ffn)
    first_cols, second_cols = FF_CHUNKS
    f = (_ffn_chunk(h, first_cols, w_gate_ref, w_up_ref, w_down_ref)
         + _ffn_chunk(h, second_cols, w_gate_ref, w_up_ref, w_down_ref))
    return x1 + _rms(f, g_post_ffn)


def _conv_neighbours(u_ref, u_before_ref, u_after_ref, u_meta_ref, j, tiles_per_seq):
    u_cur = u_ref[0].astype(F32)
    tail = BF16_SUBLANES - 1
    before = jnp.where(j == 0, u_meta_ref[0, tail:tail + 1, :], u_before_ref[0, tail:tail + 1, :])
    after = jnp.where(j == tiles_per_seq - 1, jnp.zeros_like(u_after_ref[0, 0:1, :]),
                      u_after_ref[0, 0:1, :])
    row = lax.broadcasted_iota(jnp.int32, u_cur.shape, 0)
    u_prev = jnp.where(row == 0, before.astype(F32), pltpu.roll(u_cur, 1, 0))
    u_next = jnp.where(row == FFN_TILE - 1, after.astype(F32), pltpu.roll(u_cur, FFN_TILE - 1, 0))
    return u_prev, u_cur, u_next


def _mix_ffn_kernel(*refs, tiles_per_seq, n_tiles):
    n_in = 7
    first, tile_b, tile_a = refs[0:n_in], refs[n_in:2 * n_in], refs[2 * n_in:3 * n_in]
    (conv_w_ref, w_out_ref, g_post_mix_ref, g_pre_ffn_ref, w_gate_ref, w_up_ref, w_down_ref,
     g_post_ffn_ref) = refs[3 * n_in:3 * n_in + 8]
    o_ref, x1_a, h_a, x1_b, h_b = refs[3 * n_in + 8:]
    s = pl.program_id(0)
    first_cols, second_cols = FF_CHUNKS

    def conv(tile, j):
        _, _, gb_ref, u_ref, u_before_ref, u_after_ref, u_meta_ref = tile
        u_prev, u_cur, u_next = _conv_neighbours(u_ref, u_before_ref, u_after_ref, u_meta_ref, j,
                                                 tiles_per_seq)
        return _conv_body(gb_ref[0], u_prev, u_cur, u_next, conv_w_ref[...])

    def mix(tile, conv_out, x1_scr, h_scr):
        x1, h = _mix_body(tile[0][0], tile[1][0], conv_out, w_out_ref, g_post_mix_ref[...],
                          g_pre_ffn_ref[...])
        x1_scr[...] = x1
        h_scr[...] = h

    def ffn_chunk(h_scr, cols):
        return _ffn_chunk(h_scr[...], cols, w_gate_ref, w_up_ref, w_down_ref)

    @pl.when(s == 0)
    def _():
        mix(first, conv(first, 0), x1_a, h_a)

    conv_b = conv(tile_b, lax.rem(2 * s + 1, tiles_per_seq))
    f_a = ffn_chunk(h_a, first_cols)
    mix(tile_b, conv_b, x1_b, h_b)
    f_a = f_a + ffn_chunk(h_a, second_cols)
    o_ref[0, :FFN_TILE, :] = x1_a[...] + _rms(f_a, g_post_ffn_ref[...])

    conv_a = conv(tile_a, lax.rem(jnp.minimum(2 * s + 2, n_tiles - 1), tiles_per_seq))
    f_b = ffn_chunk(h_b, first_cols)
    mix(tile_a, conv_a, x1_a, h_a)
    f_b = f_b + ffn_chunk(h_b, second_cols)
    o_ref[0, FFN_TILE:, :] = x1_b[...] + _rms(f_b, g_post_ffn_ref[...])


def _mix_ffn_meta_kernel(x_ref, attn_ref, gb_ref, u_prev_ref, u_ref, u_next_ref,
                         conv_w_ref, w_out_ref, g_post_mix_ref, g_pre_ffn_ref,
                         w_gate_ref, w_up_ref, w_down_ref, g_post_ffn_ref, o_ref):
    o_ref[...] = _mix_ffn_body(
        x_ref[...], attn_ref[...], gb_ref[...], u_prev_ref[...].astype(F32),
        u_ref[...].astype(F32), u_next_ref[...].astype(F32), conv_w_ref[...], w_out_ref,
        g_post_mix_ref[...], g_pre_ffn_ref[...], w_gate_ref, w_up_ref, w_down_ref,
        g_post_ffn_ref[...])


def _ffn_weight_specs(layer):
    return [
        _layer_spec((CONV_K, CONV_WIDTH), layer),
        _layer_spec((D_MODEL, D_MODEL), layer),
        _layer_spec((1, D_MODEL), layer),
        _layer_spec((1, D_MODEL), layer),
        _layer_spec((D_MODEL, D_FF), layer),
        _layer_spec((D_MODEL, D_FF), layer),
        _layer_spec((D_FF, D_MODEL), layer),
        _layer_spec((1, D_MODEL), layer),
    ]


def _ffn_weights(lw):
    return (lw["conv_w"], lw["w_out"], lw["g_post_mix"], lw["g_pre_ffn"], lw["w_gate"], lw["w_up"],
            lw["w_down"], lw["g_post_ffn"])


def _mix_ffn_real(x, attn, g_b, u, u_meta, lw, layer):
    b, seq, _ = x.shape
    tiles_per_seq = seq // FFN_TILE
    n_tiles = b * tiles_per_seq
    halo_per_tile = FFN_TILE // BF16_SUBLANES
    n_halo = seq // BF16_SUBLANES

    def tile_specs(tile_of_step, **kw):
        def at(fn):
            def index_map(s):
                q = tile_of_step(s)
                return fn(q // tiles_per_seq, q % tiles_per_seq)
            return index_map
        rows = lambda w: pl.BlockSpec((1, FFN_TILE, w), at(lambda i, j: (i, j, 0)), **kw)
        halo = lambda fn: pl.BlockSpec((1, BF16_SUBLANES, CONV_WIDTH), at(fn), **kw)
        return [rows(D_MODEL), rows(MLA_WIDTH), rows(CONV_WIDTH), rows(CONV_WIDTH),
                halo(lambda i, j: (i, jnp.maximum(j * halo_per_tile - 1, 0), 0)),
                halo(lambda i, j: (i, jnp.minimum((j + 1) * halo_per_tile, n_halo - 1), 0)),
                halo(lambda i, j: (i, 0, 0))]

    in_specs = (tile_specs(lambda s: 0 * s, pipeline_mode=pl.Buffered(1))
                + tile_specs(lambda s: 2 * s + 1)
                + tile_specs(lambda s: jnp.minimum(2 * s + 2, n_tiles - 1))
                + _ffn_weight_specs(layer))
    steps_per_seq = tiles_per_seq // 2
    kern = functools.partial(_mix_ffn_kernel, tiles_per_seq=tiles_per_seq, n_tiles=n_tiles)
    acts = (x, attn, g_b, u, u, u, u_meta)
    return pl.pallas_call(
        kern,
        grid=(n_tiles // 2,),
        in_specs=in_specs,
        out_specs=pl.BlockSpec((1, 2 * FFN_TILE, D_MODEL),
                               lambda s: (s // steps_per_seq, s % steps_per_seq, 0)),
        out_shape=jax.ShapeDtypeStruct((b, seq, D_MODEL), F32),
        scratch_shapes=[pltpu.VMEM((FFN_TILE, D_MODEL), F32), pltpu.VMEM((FFN_TILE, D_MODEL), BF16),
                        pltpu.VMEM((FFN_TILE, D_MODEL), F32), pltpu.VMEM((FFN_TILE, D_MODEL), BF16)],
        compiler_params=pltpu.CompilerParams(
            dimension_semantics=("arbitrary",), vmem_limit_bytes=VMEM_LIMIT),
        name="mix_ffn_real",
    )(*acts, *acts, *acts, *_ffn_weights(lw))


def _mix_ffn_meta(xm, attn, g_b, u_prev, u, u_next, lw, layer):
    r = META_ROWS
    in_specs = [_const_spec((r, D_MODEL)), _const_spec((r, MLA_WIDTH))] \
        + [_const_spec((r, CONV_WIDTH)) for _ in range(4)] + _ffn_weight_specs(layer)
    return pl.pallas_call(
        _mix_ffn_meta_kernel,
        grid=(1,),
        in_specs=in_specs,
        out_specs=pl.BlockSpec((r, D_MODEL), lambda i: (0, 0)),
        out_shape=jax.ShapeDtypeStruct((r, D_MODEL), F32),
        compiler_params=pltpu.CompilerParams(
            dimension_semantics=("arbitrary",), vmem_limit_bytes=VMEM_LIMIT),
        name="mix_ffn_meta",
    )(xm, attn, g_b, u_prev, u, u_next, *_ffn_weights(lw))


def _pack_weights(pre_mix_g, w_in, q_norm_g, w_q_up, kv_norm_g, w_kv_up, conv_w, w_out,
                  post_mix_g, pre_ffn_g, w_gate, w_up, w_down, post_ffn_g):
    i2 = Q_LORA + KV_LORA + QK_ROPE
    zpad = jnp.zeros((DEPTH, D_MODEL, LANES - QK_ROPE), F32)
    w_in_p = jnp.concatenate([w_in[:, :, :i2], zpad, w_in[:, :, i2:]], axis=2).astype(BF16)
    wq_t = jnp.swapaxes(w_q_up, 1, 2).astype(BF16)
    wkv = w_kv_up.reshape(DEPTH, KV_LORA, HEADS, QK_NOPE + V_HEAD)
    wk = wkv[..., :QK_NOPE].reshape(DEPTH, KV_LORA, HEADS * QK_NOPE).astype(BF16)
    wv_t = jnp.swapaxes(wkv[..., QK_NOPE:].reshape(DEPTH, KV_LORA, HEADS * V_HEAD), 1, 2).astype(BF16)
    row = lambda g: g.reshape(DEPTH, 1, -1).astype(F32)
    return dict(
        g_pre=row(pre_mix_g), w_in=w_in_p, g_q=row(q_norm_g), wq_t=wq_t, g_kv=row(kv_norm_g),
        wk=wk, wv_t=wv_t, conv_w=conv_w.astype(F32), w_out=w_out.astype(BF16),
        g_post_mix=row(post_mix_g), g_pre_ffn=row(pre_ffn_g), w_gate=w_gate.astype(BF16),
        w_up=w_up.astype(BF16), w_down=w_down.astype(BF16), g_post_ffn=row(post_ffn_g))


def _rope_tables(pos):
    inv_freq = np.float32(ROPE_THETA) ** (-np.arange(0, QK_ROPE, 2, dtype=np.float32) / np.float32(QK_ROPE))
    ang = (pos.astype(np.float32)[:, None] * inv_freq.astype(np.float32)[None, :]).astype(np.float64)
    c, s = np.cos(ang).astype(np.float32), np.sin(ang).astype(np.float32)
    z = np.zeros((pos.shape[0], LANES - QK_ROPE), np.float32)
    cos = np.concatenate([c, c, z], axis=1)
    sin = np.concatenate([-s, s, z], axis=1)
    return tuple(jnp.asarray(t) for t in (cos, sin, np.ascontiguousarray(c.T), np.ascontiguousarray(s.T)))


def _query_chunk(seq):
    return 1024 if seq // K_CHUNK <= K_UNROLL else 512


def _meta_to_groups(a, batches, axis):
    out, r0 = [], 0
    for b in batches:
        sl = lax.slice_in_dim(a, r0, r0 + b * N_META, axis=axis)
        shape = a.shape[:axis] + (b, N_META) + a.shape[axis + 1:]
        out.append(sl.reshape(shape))
        r0 += b * N_META
    return out


def _join_meta(parts):
    flat = [p.reshape(-1, p.shape[-1]) for p in parts]
    rows = sum(f.shape[0] for f in flat)
    flat.append(jnp.zeros((META_ROWS - rows, flat[0].shape[-1]), flat[0].dtype))
    return jnp.concatenate(flat, axis=0)


def kernel(x_prompt, x_sample, meta_tokens, pre_mix_g, w_in, q_norm_g, w_q_up, kv_norm_g, w_kv_up,
           conv_w, w_out, post_mix_g, pre_ffn_g, w_gate, w_up, w_down, post_ffn_g):
    xs = [x_prompt, x_sample]
    batches = [x.shape[0] for x in xs]
    assert sum(batches) * N_META <= META_ROWS
    for x in xs:
        assert x.shape[1] % max(ROW_TILE, 2 * FFN_TILE) == 0 and x.shape[2] == D_MODEL

    real_tabs = [_rope_tables(N_META + np.arange(x.shape[1])) for x in xs]
    meta_pos = np.arange(META_ROWS) % N_META
    meta_tabs = _rope_tables(meta_pos)

    xm = _join_meta([jnp.broadcast_to(meta_tokens.astype(F32)[None], (b, N_META, D_MODEL))
                     for b in batches])

    lw = _pack_weights(pre_mix_g, w_in, q_norm_g, w_q_up, kv_norm_g, w_kv_up, conv_w, w_out,
                       post_mix_g, pre_ffn_g, w_gate, w_up, w_down, post_ffn_g)
    for layer in range(DEPTH):
        qm, km, vm, gbm, um = _proj_meta(xm, lw, layer, meta_tabs)
        qm_g = _meta_to_groups(qm, batches, 2)
        km_g = _meta_to_groups(km, batches, 1)
        vm_g = _meta_to_groups(vm, batches, 2)
        um_g = _meta_to_groups(um, batches, 0)

        new_xs, attn_meta, u_prev_meta, u_next_meta = [], [], [], []
        for g, x in enumerate(xs):
            q_chunk = _query_chunk(x.shape[1])
            q_t, k, v_t, g_b, u = _proj_real(x, lw, layer, real_tabs[g], q_chunk)
            k_meta = jnp.pad(jnp.transpose(km_g[g], (1, 0, 2, 3)),
                             ((0, 0), (0, 0), (0, META_PAD - N_META), (0, 0)))
            v_meta_t = jnp.pad(jnp.transpose(vm_g[g], (2, 0, 1, 3)),
                               ((0, 0), (0, 0), (0, 0), (0, META_PAD - N_META)))
            q_meta_t = jnp.pad(jnp.transpose(qm_g[g], (2, 0, 1, 3)),
                               ((0, 0), (0, 0), (0, 0), (0, META_PAD - N_META)))[:, :, None]
            attn = _attention(q_t, k, v_t, k_meta, v_meta_t,
                              q_chunks_per_step=min(x.shape[1], Q_PER_STEP) // q_chunk)
            attn_m = _attention(q_meta_t, k, v_t, k_meta, v_meta_t, q_chunks_per_step=1)
            attn_meta.append(attn_m[:, :N_META])
            um_b = um_g[g]
            zero_row = jnp.zeros_like(um_b[:, :1])
            u_prev_meta.append(jnp.concatenate([zero_row, um_b[:, :-1]], axis=1))
            u_next_meta.append(jnp.concatenate([um_b[:, 1:], u[:, :1]], axis=1))
            new_xs.append(_mix_ffn_real(x, attn, g_b, u, um_b, lw, layer))

        xm = _mix_ffn_meta(xm, _join_meta(attn_meta), gbm, _join_meta(u_prev_meta), um,
                           _join_meta(u_next_meta), lw, layer)
        xs = new_xs

    return (xs[0], xs[1])
```

```python
import functools

import jax
import jax.numpy as jnp
import numpy as np
from jax import lax
from jax.experimental import pallas as pl
from jax.experimental.pallas import tpu as pltpu

F32 = jnp.float32
BF16 = jnp.bfloat16

D_MODEL = 1024
DEPTH = 2
N_META = 16
HEADS = 4
QK_NOPE = 128
QK_ROPE = 64
ROPE_HALF = QK_ROPE // 2
V_HEAD = 128
MLA_WIDTH = HEADS * V_HEAD
CONV_WIDTH = D_MODEL - MLA_WIDTH
CONV_K = 3
Q_LORA = 384
KV_LORA = 256
D_FF = 2816
ROPE_THETA = 10000.0
EPS = 1e-6
ATTN_SCALE = (QK_NOPE + QK_ROPE) ** -0.5
LOG2_E = 1.4426950408889634
Q_SCALE = ATTN_SCALE * LOG2_E

LANES = 128
BF16_SUBLANES = 16
VMEM_LIMIT = 56 * 1024 * 1024

QK_PAD = QK_NOPE + LANES
META_PAD = LANES
ROW_TILE = 512
FFN_TILE = 256
Q_PER_STEP = 8192
K_CHUNK = 512
K_UNROLL = 16
META_ROWS = 384
FF_CHUNKS = ((0, 1536), (1536, 2816))

C_Q0, C_KV0, C_KR0 = 0, Q_LORA, Q_LORA + KV_LORA
C_GB0 = C_KR0 + LANES
C_GC0 = C_GB0 + CONV_WIDTH
C_CH0 = C_GC0 + CONV_WIDTH
IN_COLS_P = C_CH0 + CONV_WIDTH
Q_HEAD_ROWS = QK_NOPE + QK_ROPE

NEG_BIG = -1e30


def _rms(x, g):
    ms = jnp.mean(x * x, axis=-1, keepdims=True)
    return x * lax.rsqrt(ms + EPS) * g


def _dot(a, b):
    return jnp.dot(a, b, preferred_element_type=F32)


def _dot_nt(a, b):
    return lax.dot_general(a, b, (((1,), (1,)), ((), ())), preferred_element_type=F32)


def _proj_body(x, g_pre, w_in_ref, g_q, wq_t_ref, g_kv, wk_ref, wv_t_ref, cos, sin, cos_t, sin_t):
    h = _rms(x, g_pre).astype(BF16)
    z = _dot(h, w_in_ref[...])
    cqn = _rms(z[:, C_Q0:C_KV0], g_q).astype(BF16)
    ckvn = _rms(z[:, C_KV0:C_KR0], g_kv).astype(BF16)
    k_r = z[:, C_KR0:C_GB0]
    lane = lax.broadcasted_iota(jnp.int32, k_r.shape, 1)
    partner = jnp.where(lane < ROPE_HALF, pltpu.roll(k_r, LANES - ROPE_HALF, 1),
                        pltpu.roll(k_r, ROPE_HALF, 1))
    k_rot = k_r * cos + partner * sin
    g_b = z[:, C_GB0:C_GC0]
    u = z[:, C_GC0:C_CH0] * z[:, C_CH0:IN_COLS_P]
    q_t = _dot_nt(wq_t_ref[...], cqn)
    k_nope = _dot(ckvn, wk_ref[...])
    v_t = _dot_nt(wv_t_ref[...], ckvn)
    q_heads = []
    for hd in range(HEADS):
        r0 = hd * Q_HEAD_ROWS
        nope = q_t[r0:r0 + QK_NOPE]
        x1 = q_t[r0 + QK_NOPE:r0 + QK_NOPE + ROPE_HALF]
        x2 = q_t[r0 + QK_NOPE + ROPE_HALF:r0 + Q_HEAD_ROWS]
        rot = jnp.concatenate([x1 * cos_t - x2 * sin_t, x2 * cos_t + x1 * sin_t,
                               jnp.zeros((QK_PAD - Q_HEAD_ROWS, q_t.shape[1]), F32)], axis=0)
        q_heads.append(((nope * Q_SCALE).astype(BF16), (rot * Q_SCALE).astype(BF16)))
    return q_heads, k_nope.astype(BF16), k_rot.astype(BF16), v_t.astype(BF16), g_b.astype(BF16), u.astype(BF16)


def _proj_kernel(x_ref, g_pre_ref, w_in_ref, g_q_ref, wq_t_ref, g_kv_ref, wk_ref, wv_t_ref,
                 cos_ref, sin_ref, cos_t_ref, sin_t_ref,
                 q_ref, k_ref, v_ref, gb_ref, u_ref):
    q_heads, k_nope, k_rot, v_t, g_b, u = _proj_body(
        x_ref[0], g_pre_ref[...], w_in_ref, g_q_ref[...], wq_t_ref, g_kv_ref[...], wk_ref,
        wv_t_ref, cos_ref[...], sin_ref[...], cos_t_ref[...], sin_t_ref[...])
    n_k = ROW_TILE // K_CHUNK
    for hd in range(HEADS):
        nope, rot = q_heads[hd]
        q_ref[0, hd, 0, :QK_NOPE, :] = nope
        q_ref[0, hd, 0, QK_NOPE:, :] = rot
        k_ref[0, hd, :, :QK_NOPE] = k_nope[:, hd * QK_NOPE:(hd + 1) * QK_NOPE]
        k_ref[0, hd, :, QK_NOPE:] = k_rot
        for c in range(n_k):
            v_ref[0, hd, c] = v_t[hd * V_HEAD:(hd + 1) * V_HEAD, c * K_CHUNK:(c + 1) * K_CHUNK]
    gb_ref[0] = g_b
    u_ref[0] = u


def _proj_meta_kernel(x_ref, g_pre_ref, w_in_ref, g_q_ref, wq_t_ref, g_kv_ref, wk_ref, wv_t_ref,
                      cos_ref, sin_ref, cos_t_ref, sin_t_ref,
                      q_ref, k_ref, v_ref, gb_ref, u_ref):
    q_heads, k_nope, k_rot, v_t, g_b, u = _proj_body(
        x_ref[...], g_pre_ref[...], w_in_ref, g_q_ref[...], wq_t_ref, g_kv_ref[...], wk_ref,
        wv_t_ref, cos_ref[...], sin_ref[...], cos_t_ref[...], sin_t_ref[...])
    for hd in range(HEADS):
        nope, rot = q_heads[hd]
        q_ref[hd, :QK_NOPE, :] = nope
        q_ref[hd, QK_NOPE:, :] = rot
        k_ref[hd, :, :QK_NOPE] = k_nope[:, hd * QK_NOPE:(hd + 1) * QK_NOPE]
        k_ref[hd, :, QK_NOPE:] = k_rot
        v_ref[hd] = v_t[hd * V_HEAD:(hd + 1) * V_HEAD, :]
    gb_ref[...] = g_b
    u_ref[...] = u


def _const_spec(shape):
    zeros = (0,) * len(shape)
    return pl.BlockSpec(shape, lambda *_: zeros, pipeline_mode=pl.Buffered(1))


def _layer_spec(shape, layer):
    zeros = (0,) * len(shape)
    return pl.BlockSpec((None,) + shape, lambda *_: (layer,) + zeros, pipeline_mode=pl.Buffered(1))


def _proj_weight_specs(layer):
    return [
        _layer_spec((1, D_MODEL), layer),
        _layer_spec((D_MODEL, IN_COLS_P), layer),
        _layer_spec((1, Q_LORA), layer),
        _layer_spec((HEADS * Q_HEAD_ROWS, Q_LORA), layer),
        _layer_spec((1, KV_LORA), layer),
        _layer_spec((KV_LORA, HEADS * QK_NOPE), layer),
        _layer_spec((HEADS * V_HEAD, KV_LORA), layer),
    ]


def _proj_real(x, lw, layer, tabs, q_chunk):
    b, seq, _ = x.shape
    nt = seq // ROW_TILE
    cos, sin, cos_t, sin_t = tabs
    in_specs = ([pl.BlockSpec((1, ROW_TILE, D_MODEL), lambda i, j: (i, j, 0))]
                + _proj_weight_specs(layer)
                + [pl.BlockSpec((ROW_TILE, LANES), lambda i, j: (j, 0)),
                   pl.BlockSpec((ROW_TILE, LANES), lambda i, j: (j, 0)),
                   pl.BlockSpec((ROPE_HALF, ROW_TILE), lambda i, j: (0, j)),
                   pl.BlockSpec((ROPE_HALF, ROW_TILE), lambda i, j: (0, j))])
    tiles_per_q = q_chunk // ROW_TILE
    n_k = ROW_TILE // K_CHUNK
    out_shape = [
        jax.ShapeDtypeStruct((b, HEADS, seq // q_chunk, QK_PAD, q_chunk), BF16),
        jax.ShapeDtypeStruct((b, HEADS, seq, QK_PAD), BF16),
        jax.ShapeDtypeStruct((b, HEADS, seq // K_CHUNK, V_HEAD, K_CHUNK), BF16),
        jax.ShapeDtypeStruct((b, seq, CONV_WIDTH), BF16),
        jax.ShapeDtypeStruct((b, seq, CONV_WIDTH), BF16),
    ]
    out_specs = [
        pl.BlockSpec((1, HEADS, 1, QK_PAD, ROW_TILE),
                     lambda i, j: (i, 0, j // tiles_per_q, 0, j % tiles_per_q)),
        pl.BlockSpec((1, HEADS, ROW_TILE, QK_PAD), lambda i, j: (i, 0, j, 0)),
        pl.BlockSpec((1, HEADS, n_k, V_HEAD, K_CHUNK), lambda i, j: (i, 0, j, 0, 0)),
        pl.BlockSpec((1, ROW_TILE, CONV_WIDTH), lambda i, j: (i, j, 0)),
        pl.BlockSpec((1, ROW_TILE, CONV_WIDTH), lambda i, j: (i, j, 0)),
    ]
    return pl.pallas_call(
        _proj_kernel,
        grid=(b, nt),
        in_specs=in_specs,
        out_specs=out_specs,
        out_shape=out_shape,
        compiler_params=pltpu.CompilerParams(
            dimension_semantics=("parallel", "parallel"), vmem_limit_bytes=VMEM_LIMIT),
        name="proj_real",
    )(x, lw["g_pre"], lw["w_in"], lw["g_q"], lw["wq_t"], lw["g_kv"], lw["wk"], lw["wv_t"],
      cos, sin, cos_t, sin_t)


def _proj_meta(xm, lw, layer, tabs):
    cos, sin, cos_t, sin_t = tabs
    r = META_ROWS
    in_specs = ([_const_spec((r, D_MODEL))] + _proj_weight_specs(layer)
                + [_const_spec((r, LANES)), _const_spec((r, LANES)),
                   _const_spec((ROPE_HALF, r)), _const_spec((ROPE_HALF, r))])
    out_shape = [
        jax.ShapeDtypeStruct((HEADS, QK_PAD, r), BF16),
        jax.ShapeDtypeStruct((HEADS, r, QK_PAD), BF16),
        jax.ShapeDtypeStruct((HEADS, V_HEAD, r), BF16),
        jax.ShapeDtypeStruct((r, CONV_WIDTH), BF16),
        jax.ShapeDtypeStruct((r, CONV_WIDTH), BF16),
    ]
    out_specs = [
        pl.BlockSpec((HEADS, QK_PAD, r), lambda i: (0, 0, 0)),
        pl.BlockSpec((HEADS, r, QK_PAD), lambda i: (0, 0, 0)),
        pl.BlockSpec((HEADS, V_HEAD, r), lambda i: (0, 0, 0)),
        pl.BlockSpec((r, CONV_WIDTH), lambda i: (0, 0)),
        pl.BlockSpec((r, CONV_WIDTH), lambda i: (0, 0)),
    ]
    return pl.pallas_call(
        _proj_meta_kernel,
        grid=(1,),
        in_specs=in_specs,
        out_specs=out_specs,
        out_shape=out_shape,
        compiler_params=pltpu.CompilerParams(
            dimension_semantics=("arbitrary",), vmem_limit_bytes=VMEM_LIMIT),
        name="proj_meta",
    )(xm, lw["g_pre"], lw["w_in"], lw["g_q"], lw["wq_t"], lw["g_kv"], lw["wk"], lw["wv_t"],
      cos, sin, cos_t, sin_t)


def _attn_kernel(q_ref, k_ref, v_ref, km_ref, vm_ref, o_ref, s_ref, sm_ref, acc_ref,
                 *, n_qc, n_kc, q_chunk, unroll):
    meta_valid = lax.broadcasted_iota(jnp.int32, (META_PAD, q_chunk), 0) < N_META
    n_groups = n_kc // unroll

    def ones_rows(n):
        first = lax.broadcasted_iota(jnp.int32, (BF16_SUBLANES, n), 0) == 0
        return jnp.where(first, 1.0, 0.0).astype(BF16)

    def put_scores(slot, kc, qc):
        k = k_ref[0, 0, pl.ds(pl.multiple_of(kc * K_CHUNK, K_CHUNK), K_CHUNK), :]
        s = _dot(k, q_ref[0, 0, qc])
        s_ref[slot] = s
        return jnp.max(s, axis=0, keepdims=True)

    def with_ones(v_t):
        return jnp.concatenate([v_t, ones_rows(v_t.shape[1])], axis=0)

    def accumulate(s, s_max, v_t, m, acc):
        m_new = jnp.maximum(m, s_max)
        p = jnp.exp2(s - m_new).astype(BF16)
        return m_new, jnp.exp2(m - m_new) * acc + _dot(with_ones(v_t), p)

    def finalize(acc, qc):
        out = (acc[:V_HEAD] / acc[V_HEAD:V_HEAD + 1]).T
        o_ref[0, pl.ds(pl.multiple_of(qc * q_chunk, q_chunk), q_chunk), :] = out.astype(o_ref.dtype)

    def k_group(base, qc, q_next, carry, first, last):
        m, acc, s_max = carry
        for t in range(unroll):
            final = last and t == unroll - 1
            if not final:
                next_max = put_scores((t + 1) % 2, base + t + 1, qc)
            elif q_next is not None:
                next_max = put_scores(0, 0, q_next)
            else:
                next_max = s_max
            if first and t == 0 and n_qc > 1:
                finalize(acc_ref[...], jnp.maximum(qc - 1, 0))
            if last and t == 0:
                s_meta = jnp.where(meta_valid, _dot(km_ref[0, 0], q_ref[0, 0, qc]), NEG_BIG)
                sm_ref[...] = s_meta
                meta_max = jnp.max(s_meta, axis=0, keepdims=True)
            m, acc = accumulate(s_ref[t % 2], s_max, v_ref[0, 0, base + t], m, acc)
            if last and t == 1:
                m, acc = accumulate(sm_ref[...], meta_max, vm_ref[0, 0], m, acc)
            s_max = next_max
        return m, acc, s_max

    def q_chunk_body(qc, q_next, s_max):
        m0 = jnp.full((1, q_chunk), NEG_BIG, F32)
        acc0 = jnp.zeros((V_HEAD + BF16_SUBLANES, q_chunk), F32)
        carry = (m0, acc0, s_max)
        if n_groups > 1:
            carry = k_group(0, qc, q_next, carry, True, False)
        if n_groups > 2:
            carry = lax.fori_loop(
                1, n_groups - 1, lambda g, c: k_group(g * unroll, qc, q_next, c, False, False), carry)
        _, acc, s_max = k_group((n_groups - 1) * unroll, qc, q_next, carry, n_groups == 1, True)
        if n_qc == 1:
            finalize(acc, qc)
        else:
            acc_ref[...] = acc
        return s_max

    s_max = put_scores(0, 0, 0)
    if n_qc == 1:
        q_chunk_body(0, None, s_max)
    else:
        acc_ref[...] = jnp.ones(acc_ref.shape, F32)
        chunks_per_trip = 2 if n_qc % 2 == 0 and n_qc > 2 else 1

        def q_trip(i, s_max):
            for c in range(chunks_per_trip):
                qc = chunks_per_trip * i + c
                s_max = q_chunk_body(qc, jnp.minimum(qc + 1, n_qc - 1), s_max)
            return s_max

        lax.fori_loop(0, n_qc // chunks_per_trip, q_trip, s_max)
        finalize(acc_ref[...], n_qc - 1)


def _attention(q_t, k, v_t, k_meta, v_meta_t, *, q_chunks_per_step):
    b, _, n_q_total, _, q_chunk = q_t.shape
    seq_k = k.shape[2]
    n_kc = seq_k // K_CHUNK
    unroll = min(K_UNROLL, n_kc)
    assert unroll % 2 == 0 and n_kc % unroll == 0
    n_qc = q_chunks_per_step
    n_steps = n_q_total // n_qc
    tq = n_qc * q_chunk
    kern = functools.partial(_attn_kernel, n_qc=n_qc, n_kc=n_kc, q_chunk=q_chunk, unroll=unroll)
    return pl.pallas_call(
        kern,
        grid=(b, HEADS, n_steps),
        in_specs=[
            pl.BlockSpec((1, 1, n_qc, QK_PAD, q_chunk), lambda i, h, j: (i, h, j, 0, 0)),
            pl.BlockSpec((1, 1, seq_k, QK_PAD), lambda i, h, j: (i, h, 0, 0)),
            pl.BlockSpec((1, 1, n_kc, V_HEAD, K_CHUNK), lambda i, h, j: (i, h, 0, 0, 0)),
            pl.BlockSpec((1, 1, META_PAD, QK_PAD), lambda i, h, j: (i, h, 0, 0)),
            pl.BlockSpec((1, 1, V_HEAD, META_PAD), lambda i, h, j: (i, h, 0, 0)),
        ],
        out_specs=pl.BlockSpec((1, tq, V_HEAD), lambda i, h, j: (i, j, h)),
        out_shape=jax.ShapeDtypeStruct((b, n_q_total * q_chunk, MLA_WIDTH), BF16),
        scratch_shapes=[pltpu.VMEM((2, K_CHUNK, q_chunk), F32), pltpu.VMEM((META_PAD, q_chunk), F32),
                        pltpu.VMEM((V_HEAD + BF16_SUBLANES, q_chunk), F32)],
        compiler_params=pltpu.CompilerParams(
            dimension_semantics=("parallel", "parallel", "parallel"), vmem_limit_bytes=VMEM_LIMIT),
        name="attention",
    )(q_t, k, v_t, k_meta, v_meta_t)


def _conv_body(g_b, u_prev, u_cur, u_next, conv_w):
    y = u_prev * conv_w[0:1] + u_cur * conv_w[1:2] + u_next * conv_w[2:3]
    return (g_b.astype(F32) * y).astype(BF16)


def _mix_body(x, attn, conv, w_out_ref, g_post_mix, g_pre_ffn):
    mix = _dot(attn, w_out_ref[:MLA_WIDTH, :]) + _dot(conv, w_out_ref[MLA_WIDTH:, :])
    x1 = x + _rms(mix, g_post_mix)
    return x1, _rms(x1, g_pre_ffn).astype(BF16)


def _ffn_chunk(h, cols, w_gate_ref, w_up_ref, w_down_ref):
    c0, c1 = cols
    gate = _dot(h, w_gate_ref[:, c0:c1])
    up = _dot(h, w_up_ref[:, c0:c1])
    act = (gate * (0.5 + 0.5 * jnp.tanh(0.5 * gate)) * up).astype(BF16)
    return _dot(act, w_down_ref[c0:c1, :])


def _mix_ffn_body(x, attn, g_b, u_prev, u_cur, u_next, conv_w, w_out_ref, g_post_mix, g_pre_ffn,
                  w_gate_ref, w_up_ref, w_down_ref, g_post_ffn):
    conv = _conv_body(g_b, u_prev, u_cur, u_next, conv_w)
    x1, h = _mix_body(x, attn, conv, w_out_ref, g_post_mix, g_pre_ffn)
    first_cols, second_cols = FF_CHUNKS
    f = (_ffn_chunk(h, first_cols, w_gate_ref, w_up_ref, w_down_ref)
         + _ffn_chunk(h, second_cols, w_gate_ref, w_up_ref, w_down_ref))
    return x1 + _rms(f, g_post_ffn)


def _conv_neighbours(u_ref, u_before_ref, u_after_ref, u_meta_ref, j, tiles_per_seq):
    u_cur = u_ref[0].astype(F32)
    tail = BF16_SUBLANES - 1
    before = jnp.where(j == 0, u_meta_ref[0, tail:tail + 1, :], u_before_ref[0, tail:tail + 1, :])
    after = jnp.where(j == tiles_per_seq - 1, jnp.zeros_like(u_after_ref[0, 0:1, :]),
                      u_after_ref[0, 0:1, :])
    row = lax.broadcasted_iota(jnp.int32, u_cur.shape, 0)
    u_prev = jnp.where(row == 0, before.astype(F32), pltpu.roll(u_cur, 1, 0))
    u_next = jnp.where(row == FFN_TILE - 1, after.astype(F32), pltpu.roll(u_cur, FFN_TILE - 1, 0))
    return u_prev, u_cur, u_next


def _mix_ffn_kernel(*refs, tiles_per_seq, n_tiles):
    n_in = 7
    first, tile_b, tile_a = refs[0:n_in], refs[n_in:2 * n_in], refs[2 * n_in:3 * n_in]
    (conv_w_ref, w_out_ref, g_post_mix_ref, g_pre_ffn_ref, w_gate_ref, w_up_ref, w_down_ref,
     g_post_ffn_ref) = refs[3 * n_in:3 * n_in + 8]
    o_ref, x1_a, h_a, x1_b, h_b = refs[3 * n_in + 8:]
    s = pl.program_id(0)
    first_cols, second_cols = FF_CHUNKS

    def conv(tile, j):
        _, _, gb_ref, u_ref, u_before_ref, u_after_ref, u_meta_ref = tile
        u_prev, u_cur, u_next = _conv_neighbours(u_ref, u_before_ref, u_after_ref, u_meta_ref, j,
                                                 tiles_per_seq)
        return _conv_body(gb_ref[0], u_prev, u_cur, u_next, conv_w_ref[...])

    def mix(tile, conv_out, x1_scr, h_scr):
        x1, h = _mix_body(tile[0][0], tile[1][0], conv_out, w_out_ref, g_post_mix_ref[...],
                          g_pre_ffn_ref[...])
        x1_scr[...] = x1
        h_scr[...] = h

    def ffn_chunk(h_scr, cols):
        return _ffn_chunk(h_scr[...], cols, w_gate_ref, w_up_ref, w_down_ref)

    @pl.when(s == 0)
    def _():
        mix(first, conv(first, 0), x1_a, h_a)

    conv_b = conv(tile_b, lax.rem(2 * s + 1, tiles_per_seq))
    f_a = ffn_chunk(h_a, first_cols)
    mix(tile_b, conv_b, x1_b, h_b)
    f_a = f_a + ffn_chunk(h_a, second_cols)
    o_ref[0, :FFN_TILE, :] = x1_a[...] + _rms(f_a, g_post_ffn_ref[...])

    conv_a = conv(tile_a, lax.rem(jnp.minimum(2 * s + 2, n_tiles - 1), tiles_per_seq))
    f_b = ffn_chunk(h_b, first_cols)
    mix(tile_a, conv_a, x1_a, h_a)
    f_b = f_b + ffn_chunk(h_b, second_cols)
    o_ref[0, FFN_TILE:, :] = x1_b[...] + _rms(f_b, g_post_ffn_ref[...])


def _mix_ffn_meta_kernel(x_ref, attn_ref, gb_ref, u_prev_ref, u_ref, u_next_ref,
                         conv_w_ref, w_out_ref, g_post_mix_ref, g_pre_ffn_ref,
                         w_gate_ref, w_up_ref, w_down_ref, g_post_ffn_ref, o_ref):
    o_ref[...] = _mix_ffn_body(
        x_ref[...], attn_ref[...], gb_ref[...], u_prev_ref[...].astype(F32),
        u_ref[...].astype(F32), u_next_ref[...].astype(F32), conv_w_ref[...], w_out_ref,
        g_post_mix_ref[...], g_pre_ffn_ref[...], w_gate_ref, w_up_ref, w_down_ref,
        g_post_ffn_ref[...])


def _ffn_weight_specs(layer):
    return [
        _layer_spec((CONV_K, CONV_WIDTH), layer),
        _layer_spec((D_MODEL, D_MODEL), layer),
        _layer_spec((1, D_MODEL), layer),
        _layer_spec((1, D_MODEL), layer),
        _layer_spec((D_MODEL, D_FF), layer),
        _layer_spec((D_MODEL, D_FF), layer),
        _layer_spec((D_FF, D_MODEL), layer),
        _layer_spec((1, D_MODEL), layer),
    ]


def _ffn_weights(lw):
    return (lw["conv_w"], lw["w_out"], lw["g_post_mix"], lw["g_pre_ffn"], lw["w_gate"], lw["w_up"],
            lw["w_down"], lw["g_post_ffn"])


def _mix_ffn_real(x, attn, g_b, u, u_meta, lw, layer):
    b, seq, _ = x.shape
    tiles_per_seq = seq // FFN_TILE
    n_tiles = b * tiles_per_seq
    halo_per_tile = FFN_TILE // BF16_SUBLANES
    n_halo = seq // BF16_SUBLANES

    def tile_specs(tile_of_step, **kw):
        def at(fn):
            def index_map(s):
                q = tile_of_step(s)
                return fn(q // tiles_per_seq, q % tiles_per_seq)
            return index_map
        rows = lambda w: pl.BlockSpec((1, FFN_TILE, w), at(lambda i, j: (i, j, 0)), **kw)
        halo = lambda fn: pl.BlockSpec((1, BF16_SUBLANES, CONV_WIDTH), at(fn), **kw)
        return [rows(D_MODEL), rows(MLA_WIDTH), rows(CONV_WIDTH), rows(CONV_WIDTH),
                halo(lambda i, j: (i, jnp.maximum(j * halo_per_tile - 1, 0), 0)),
                halo(lambda i, j: (i, jnp.minimum((j + 1) * halo_per_tile, n_halo - 1), 0)),
                halo(lambda i, j: (i, 0, 0))]

    in_specs = (tile_specs(lambda s: 0 * s, pipeline_mode=pl.Buffered(1))
                + tile_specs(lambda s: 2 * s + 1)
                + tile_specs(lambda s: jnp.minimum(2 * s + 2, n_tiles - 1))
                + _ffn_weight_specs(layer))
    steps_per_seq = tiles_per_seq // 2
    kern = functools.partial(_mix_ffn_kernel, tiles_per_seq=tiles_per_seq, n_tiles=n_tiles)
    acts = (x, attn, g_b, u, u, u, u_meta)
    return pl.pallas_call(
        kern,
        grid=(n_tiles // 2,),
        in_specs=in_specs,
        out_specs=pl.BlockSpec((1, 2 * FFN_TILE, D_MODEL),
                               lambda s: (s // steps_per_seq, s % steps_per_seq, 0)),
        out_shape=jax.ShapeDtypeStruct((b, seq, D_MODEL), F32),
        scratch_shapes=[pltpu.VMEM((FFN_TILE, D_MODEL), F32), pltpu.VMEM((FFN_TILE, D_MODEL), BF16),
                        pltpu.VMEM((FFN_TILE, D_MODEL), F32), pltpu.VMEM((FFN_TILE, D_MODEL), BF16)],
        compiler_params=pltpu.CompilerParams(
            dimension_semantics=("arbitrary",), vmem_limit_bytes=VMEM_LIMIT),
        name="mix_ffn_real",
    )(*acts, *acts, *acts, *_ffn_weights(lw))


def _mix_ffn_meta(xm, attn, g_b, u_prev, u, u_next, lw, layer):
    r = META_ROWS
    in_specs = [_const_spec((r, D_MODEL)), _const_spec((r, MLA_WIDTH))] \
        + [_const_spec((r, CONV_WIDTH)) for _ in range(4)] + _ffn_weight_specs(layer)
    return pl.pallas_call(
        _mix_ffn_meta_kernel,
        grid=(1,),
        in_specs=in_specs,
        out_specs=pl.BlockSpec((r, D_MODEL), lambda i: (0, 0)),
        out_shape=jax.ShapeDtypeStruct((r, D_MODEL), F32),
        compiler_params=pltpu.CompilerParams(
            dimension_semantics=("arbitrary",), vmem_limit_bytes=VMEM_LIMIT),
        name="mix_ffn_meta",
    )(xm, attn, g_b, u_prev, u, u_next, *_ffn_weights(lw))


def _pack_weights(pre_mix_g, w_in, q_norm_g, w_q_up, kv_norm_g, w_kv_up, conv_w, w_out,
                  post_mix_g, pre_ffn_g, w_gate, w_up, w_down, post_ffn_g):
    i2 = Q_LORA + KV_LORA + QK_ROPE
    zpad = jnp.zeros((DEPTH, D_MODEL, LANES - QK_ROPE), F32)
    w_in_p = jnp.concatenate([w_in[:, :, :i2], zpad, w_in[:, :, i2:]], axis=2).astype(BF16)
    wq_t = jnp.swapaxes(w_q_up, 1, 2).astype(BF16)
    wkv = w_kv_up.reshape(DEPTH, KV_LORA, HEADS, QK_NOPE + V_HEAD)
    wk = wkv[..., :QK_NOPE].reshape(DEPTH, KV_LORA, HEADS * QK_NOPE).astype(BF16)
    wv_t = jnp.swapaxes(wkv[..., QK_NOPE:].reshape(DEPTH, KV_LORA, HEADS * V_HEAD), 1, 2).astype(BF16)
    row = lambda g: g.reshape(DEPTH, 1, -1).astype(F32)
    return dict(
        g_pre=row(pre_mix_g), w_in=w_in_p, g_q=row(q_norm_g), wq_t=wq_t, g_kv=row(kv_norm_g),
        wk=wk, wv_t=wv_t, conv_w=conv_w.astype(F32), w_out=w_out.astype(BF16),
        g_post_mix=row(post_mix_g), g_pre_ffn=row(pre_ffn_g), w_gate=w_gate.astype(BF16),
        w_up=w_up.astype(BF16), w_down=w_down.astype(BF16), g_post_ffn=row(post_ffn_g))


def _rope_tables(pos):
    inv_freq = np.float32(ROPE_THETA) ** (-np.arange(0, QK_ROPE, 2, dtype=np.float32) / np.float32(QK_ROPE))
    ang = (pos.astype(np.float32)[:, None] * inv_freq.astype(np.float32)[None, :]).astype(np.float64)
    c, s = np.cos(ang).astype(np.float32), np.sin(ang).astype(np.float32)
    z = np.zeros((pos.shape[0], LANES - QK_ROPE), np.float32)
    cos = np.concatenate([c, c, z], axis=1)
    sin = np.concatenate([-s, s, z], axis=1)
    return tuple(jnp.asarray(t) for t in (cos, sin, np.ascontiguousarray(c.T), np.ascontiguousarray(s.T)))


def _query_chunk(seq):
    return 1024 if seq // K_CHUNK <= K_UNROLL else 512


def _meta_to_groups(a, batches, axis):
    out, r0 = [], 0
    for b in batches:
        sl = lax.slice_in_dim(a, r0, r0 + b * N_META, axis=axis)
        shape = a.shape[:axis] + (b, N_META) + a.shape[axis + 1:]
        out.append(sl.reshape(shape))
        r0 += b * N_META
    return out


def _join_meta(parts):
    flat = [p.reshape(-1, p.shape[-1]) for p in parts]
    rows = sum(f.shape[0] for f in flat)
    flat.append(jnp.zeros((META_ROWS - rows, flat[0].shape[-1]), flat[0].dtype))
    return jnp.concatenate(flat, axis=0)


def kernel(x_prompt, x_sample, meta_tokens, pre_mix_g, w_in, q_norm_g, w_q_up, kv_norm_g, w_kv_up,
           conv_w, w_out, post_mix_g, pre_ffn_g, w_gate, w_up, w_down, post_ffn_g):
    xs = [x_prompt, x_sample]
    batches = [x.shape[0] for x in xs]
    assert sum(batches) * N_META <= META_ROWS
    for x in xs:
        assert x.shape[1] % max(ROW_TILE, 2 * FFN_TILE) == 0 and x.shape[2] == D_MODEL

    real_tabs = [_rope_tables(N_META + np.arange(x.shape[1])) for x in xs]
    meta_pos = np.arange(META_ROWS) % N_META
    meta_tabs = _rope_tables(meta_pos)

    xm = _join_meta([jnp.broadcast_to(meta_tokens.astype(F32)[None], (b, N_META, D_MODEL))
                     for b in batches])

    lw = _pack_weights(pre_mix_g, w_in, q_norm_g, w_q_up, kv_norm_g, w_kv_up, conv_w, w_out,
                       post_mix_g, pre_ffn_g, w_gate, w_up, w_down, post_ffn_g)
    for layer in range(DEPTH):
        qm, km, vm, gbm, um = _proj_meta(xm, lw, layer, meta_tabs)
        qm_g = _meta_to_groups(qm, batches, 2)
        km_g = _meta_to_groups(km, batches, 1)
        vm_g = _meta_to_groups(vm, batches, 2)
        um_g = _meta_to_groups(um, batches, 0)

        new_xs, attn_meta, u_prev_meta, u_next_meta = [], [], [], []
        for g, x in enumerate(xs):
            q_chunk = _query_chunk(x.shape[1])
            q_t, k, v_t, g_b, u = _proj_real(x, lw, layer, real_tabs[g], q_chunk)
            k_meta = jnp.pad(jnp.transpose(km_g[g], (1, 0, 2, 3)),
                             ((0, 0), (0, 0), (0, META_PAD - N_META), (0, 0)))
            v_meta_t = jnp.pad(jnp.transpose(vm_g[g], (2, 0, 1, 3)),
                               ((0, 0), (0, 0), (0, 0), (0, META_PAD - N_META)))
            q_meta_t = jnp.pad(jnp.transpose(qm_g[g], (2, 0, 1, 3)),
                               ((0, 0), (0, 0), (0, 0), (0, META_PAD - N_META)))[:, :, None]
            attn = _attention(q_t, k, v_t, k_meta, v_meta_t,
                              q_chunks_per_step=min(x.shape[1], Q_PER_STEP) // q_chunk)
            attn_m = _attention(q_meta_t, k, v_t, k_meta, v_meta_t, q_chunks_per_step=1)
            attn_meta.append(attn_m[:, :N_META])
            um_b = um_g[g]
            zero_row = jnp.zeros_like(um_b[:, :1])
            u_prev_meta.append(jnp.concatenate([zero_row, um_b[:, :-1]], axis=1))
            u_next_meta.append(jnp.concatenate([um_b[:, 1:], u[:, :1]], axis=1))
            new_xs.append(_mix_ffn_real(x, attn, g_b, u, um_b, lw, layer))

        xm = _mix_ffn_meta(xm, _join_meta(attn_meta), gbm, _join_meta(u_prev_meta), um,
                           _join_meta(u_next_meta), lw, layer)
        xs = new_xs

    return (xs[0], xs[1])
```

```python
import functools

import jax
import jax.numpy as jnp
import numpy as np
from jax import lax
from jax.experimental import pallas as pl
from jax.experimental.pallas import tpu as pltpu

F32 = jnp.float32
BF16 = jnp.bfloat16

D_MODEL = 1024
DEPTH = 2
N_META = 16
HEADS = 4
QK_NOPE = 128
QK_ROPE = 64
ROPE_HALF = QK_ROPE // 2
V_HEAD = 128
MLA_WIDTH = HEADS * V_HEAD
CONV_WIDTH = D_MODEL - MLA_WIDTH
CONV_K = 3
Q_LORA = 384
KV_LORA = 256
D_FF = 2816
ROPE_THETA = 10000.0
EPS = 1e-6
ATTN_SCALE = (QK_NOPE + QK_ROPE) ** -0.5
LOG2_E = 1.4426950408889634
Q_SCALE = ATTN_SCALE * LOG2_E

LANES = 128
BF16_SUBLANES = 16
VMEM_LIMIT = 56 * 1024 * 1024

QK_PAD = QK_NOPE + LANES
META_PAD = LANES
ROW_TILE = 512
FFN_TILE = 256
Q_PER_STEP = 8192
K_CHUNK = 512
K_UNROLL = 16
META_ROWS = 384
FF_CHUNKS = ((0, 1536), (1536, 2816))

C_Q0, C_KV0, C_KR0 = 0, Q_LORA, Q_LORA + KV_LORA
C_GB0 = C_KR0 + LANES
C_GC0 = C_GB0 + CONV_WIDTH
C_CH0 = C_GC0 + CONV_WIDTH
IN_COLS_P = C_CH0 + CONV_WIDTH
Q_HEAD_ROWS = QK_NOPE + QK_ROPE

NEG_BIG = -1e30


def _rms(x, g):
    ms = jnp.mean(x * x, axis=-1, keepdims=True)
    return x * lax.rsqrt(ms + EPS) * g


def _dot(a, b):
    return jnp.dot(a, b, preferred_element_type=F32)


def _dot_nt(a, b):
    return lax.dot_general(a, b, (((1,), (1,)), ((), ())), preferred_element_type=F32)


def _proj_body(x, g_pre, w_in_ref, g_q, wq_t_ref, g_kv, wk_ref, wv_t_ref, cos, sin, cos_t, sin_t):
    h = _rms(x, g_pre).astype(BF16)
    z = _dot(h, w_in_ref[...])
    cqn = _rms(z[:, C_Q0:C_KV0], g_q).astype(BF16)
    ckvn = _rms(z[:, C_KV0:C_KR0], g_kv).astype(BF16)
    k_r = z[:, C_KR0:C_GB0]
    lane = lax.broadcasted_iota(jnp.int32, k_r.shape, 1)
    partner = jnp.where(lane < ROPE_HALF, pltpu.roll(k_r, LANES - ROPE_HALF, 1),
                        pltpu.roll(k_r, ROPE_HALF, 1))
    k_rot = k_r * cos + partner * sin
    g_b = z[:, C_GB0:C_GC0]
    u = z[:, C_GC0:C_CH0] * z[:, C_CH0:IN_COLS_P]
    q_t = _dot_nt(wq_t_ref[...], cqn)
    k_nope = _dot(ckvn, wk_ref[...])
    v_t = _dot_nt(wv_t_ref[...], ckvn)
    q_heads = []
    for hd in range(HEADS):
        r0 = hd * Q_HEAD_ROWS
        nope = q_t[r0:r0 + QK_NOPE]
        x1 = q_t[r0 + QK_NOPE:r0 + QK_NOPE + ROPE_HALF]
        x2 = q_t[r0 + QK_NOPE + ROPE_HALF:r0 + Q_HEAD_ROWS]
        rot = jnp.concatenate([x1 * cos_t - x2 * sin_t, x2 * cos_t + x1 * sin_t,
                               jnp.zeros((QK_PAD - Q_HEAD_ROWS, q_t.shape[1]), F32)], axis=0)
        q_heads.append(((nope * Q_SCALE).astype(BF16), (rot * Q_SCALE).astype(BF16)))
    return q_heads, k_nope.astype(BF16), k_rot.astype(BF16), v_t.astype(BF16), g_b.astype(BF16), u.astype(BF16)


def _proj_kernel(x_ref, g_pre_ref, w_in_ref, g_q_ref, wq_t_ref, g_kv_ref, wk_ref, wv_t_ref,
                 cos_ref, sin_ref, cos_t_ref, sin_t_ref,
                 q_ref, k_ref, v_ref, gb_ref, u_ref):
    q_heads, k_nope, k_rot, v_t, g_b, u = _proj_body(
        x_ref[0], g_pre_ref[...], w_in_ref, g_q_ref[...], wq_t_ref, g_kv_ref[...], wk_ref,
        wv_t_ref, cos_ref[...], sin_ref[...], cos_t_ref[...], sin_t_ref[...])
    n_k = ROW_TILE // K_CHUNK
    for hd in range(HEADS):
        nope, rot = q_heads[hd]
        q_ref[0, hd, 0, :QK_NOPE, :] = nope
        q_ref[0, hd, 0, QK_NOPE:, :] = rot
        k_ref[0, hd, :, :QK_NOPE] = k_nope[:, hd * QK_NOPE:(hd + 1) * QK_NOPE]
        k_ref[0, hd, :, QK_NOPE:] = k_rot
        for c in range(n_k):
            v_ref[0, hd, c] = v_t[hd * V_HEAD:(hd + 1) * V_HEAD, c * K_CHUNK:(c + 1) * K_CHUNK]
    gb_ref[0] = g_b
    u_ref[0] = u


def _proj_meta_kernel(x_ref, g_pre_ref, w_in_ref, g_q_ref, wq_t_ref, g_kv_ref, wk_ref, wv_t_ref,
                      cos_ref, sin_ref, cos_t_ref, sin_t_ref,
                      q_ref, k_ref, v_ref, gb_ref, u_ref):
    q_heads, k_nope, k_rot, v_t, g_b, u = _proj_body(
        x_ref[...], g_pre_ref[...], w_in_ref, g_q_ref[...], wq_t_ref, g_kv_ref[...], wk_ref,
        wv_t_ref, cos_ref[...], sin_ref[...], cos_t_ref[...], sin_t_ref[...])
    for hd in range(HEADS):
        nope, rot = q_heads[hd]
        q_ref[hd, :QK_NOPE, :] = nope
        q_ref[hd, QK_NOPE:, :] = rot
        k_ref[hd, :, :QK_NOPE] = k_nope[:, hd * QK_NOPE:(hd + 1) * QK_NOPE]
        k_ref[hd, :, QK_NOPE:] = k_rot
        v_ref[hd] = v_t[hd * V_HEAD:(hd + 1) * V_HEAD, :]
    gb_ref[...] = g_b
    u_ref[...] = u


def _const_spec(shape):
    zeros = (0,) * len(shape)
    return pl.BlockSpec(shape, lambda *_: zeros, pipeline_mode=pl.Buffered(1))


def _layer_spec(shape, layer):
    zeros = (0,) * len(shape)
    return pl.BlockSpec((None,) + shape, lambda *_: (layer,) + zeros, pipeline_mode=pl.Buffered(1))


def _proj_weight_specs(layer):
    return [
        _layer_spec((1, D_MODEL), layer),
        _layer_spec((D_MODEL, IN_COLS_P), layer),
        _layer_spec((1, Q_LORA), layer),
        _layer_spec((HEADS * Q_HEAD_ROWS, Q_LORA), layer),
        _layer_spec((1, KV_LORA), layer),
        _layer_spec((KV_LORA, HEADS * QK_NOPE), layer),
        _layer_spec((HEADS * V_HEAD, KV_LORA), layer),
    ]


def _proj_real(x, lw, layer, tabs, q_chunk):
    b, seq, _ = x.shape
    nt = seq // ROW_TILE
    cos, sin, cos_t, sin_t = tabs
    in_specs = ([pl.BlockSpec((1, ROW_TILE, D_MODEL), lambda i, j: (i, j, 0))]
                + _proj_weight_specs(layer)
                + [pl.BlockSpec((ROW_TILE, LANES), lambda i, j: (j, 0)),
                   pl.BlockSpec((ROW_TILE, LANES), lambda i, j: (j, 0)),
                   pl.BlockSpec((ROPE_HALF, ROW_TILE), lambda i, j: (0, j)),
                   pl.BlockSpec((ROPE_HALF, ROW_TILE), lambda i, j: (0, j))])
    tiles_per_q = q_chunk // ROW_TILE
    n_k = ROW_TILE // K_CHUNK
    out_shape = [
        jax.ShapeDtypeStruct((b, HEADS, seq // q_chunk, QK_PAD, q_chunk), BF16),
        jax.ShapeDtypeStruct((b, HEADS, seq, QK_PAD), BF16),
        jax.ShapeDtypeStruct((b, HEADS, seq // K_CHUNK, V_HEAD, K_CHUNK), BF16),
        jax.ShapeDtypeStruct((b, seq, CONV_WIDTH), BF16),
        jax.ShapeDtypeStruct((b, seq, CONV_WIDTH), BF16),
    ]
    out_specs = [
        pl.BlockSpec((1, HEADS, 1, QK_PAD, ROW_TILE),
                     lambda i, j: (i, 0, j // tiles_per_q, 0, j % tiles_per_q)),
        pl.BlockSpec((1, HEADS, ROW_TILE, QK_PAD), lambda i, j: (i, 0, j, 0)),
        pl.BlockSpec((1, HEADS, n_k, V_HEAD, K_CHUNK), lambda i, j: (i, 0, j, 0, 0)),
        pl.BlockSpec((1, ROW_TILE, CONV_WIDTH), lambda i, j: (i, j, 0)),
        pl.BlockSpec((1, ROW_TILE, CONV_WIDTH), lambda i, j: (i, j, 0)),
    ]
    return pl.pallas_call(
        _proj_kernel,
        grid=(b, nt),
        in_specs=in_specs,
        out_specs=out_specs,
        out_shape=out_shape,
        compiler_params=pltpu.CompilerParams(
            dimension_semantics=("parallel", "parallel"), vmem_limit_bytes=VMEM_LIMIT),
        name="proj_real",
    )(x, lw["g_pre"], lw["w_in"], lw["g_q"], lw["wq_t"], lw["g_kv"], lw["wk"], lw["wv_t"],
      cos, sin, cos_t, sin_t)


def _proj_meta(xm, lw, layer, tabs):
    cos, sin, cos_t, sin_t = tabs
    r = META_ROWS
    in_specs = ([_const_spec((r, D_MODEL))] + _proj_weight_specs(layer)
                + [_const_spec((r, LANES)), _const_spec((r, LANES)),
                   _const_spec((ROPE_HALF, r)), _const_spec((ROPE_HALF, r))])
    out_shape = [
        jax.ShapeDtypeStruct((HEADS, QK_PAD, r), BF16),
        jax.ShapeDtypeStruct((HEADS, r, QK_PAD), BF16),
        jax.ShapeDtypeStruct((HEADS, V_HEAD, r), BF16),
        jax.ShapeDtypeStruct((r, CONV_WIDTH), BF16),
        jax.ShapeDtypeStruct((r, CONV_WIDTH), BF16),
    ]
    out_specs = [
        pl.BlockSpec((HEADS, QK_PAD, r), lambda i: (0, 0, 0)),
        pl.BlockSpec((HEADS, r, QK_PAD), lambda i: (0, 0, 0)),
        pl.BlockSpec((HEADS, V_HEAD, r), lambda i: (0, 0, 0)),
        pl.BlockSpec((r, CONV_WIDTH), lambda i: (0, 0)),
        pl.BlockSpec((r, CONV_WIDTH), lambda i: (0, 0)),
    ]
    return pl.pallas_call(
        _proj_meta_kernel,
        grid=(1,),
        in_specs=in_specs,
        out_specs=out_specs,
        out_shape=out_shape,
        compiler_params=pltpu.CompilerParams(
            dimension_semantics=("arbitrary",), vmem_limit_bytes=VMEM_LIMIT),
        name="proj_meta",
    )(xm, lw["g_pre"], lw["w_in"], lw["g_q"], lw["wq_t"], lw["g_kv"], lw["wk"], lw["wv_t"],
      cos, sin, cos_t, sin_t)


def _attn_kernel(q_ref, k_ref, v_ref, km_ref, vm_ref, o_ref, s_ref, sm_ref, acc_ref,
                 *, n_qc, n_kc, q_chunk, unroll):
    meta_valid = lax.broadcasted_iota(jnp.int32, (META_PAD, q_chunk), 0) < N_META
    n_groups = n_kc // unroll

    def ones_rows(n):
        first = lax.broadcasted_iota(jnp.int32, (BF16_SUBLANES, n), 0) == 0
        return jnp.where(first, 1.0, 0.0).astype(BF16)

    def put_scores(slot, kc, qc):
        k = k_ref[0, 0, pl.ds(pl.multiple_of(kc * K_CHUNK, K_CHUNK), K_CHUNK), :]
        s = _dot(k, q_ref[0, 0, qc])
        s_ref[slot] = s
        return jnp.max(s, axis=0, keepdims=True)

    def with_ones(v_t):
        return jnp.concatenate([v_t, ones_rows(v_t.shape[1])], axis=0)

    def accumulate(s, s_max, v_t, m, acc):
        m_new = jnp.maximum(m, s_max)
        p = jnp.exp2(s - m_new).astype(BF16)
        return m_new, jnp.exp2(m - m_new) * acc + _dot(with_ones(v_t), p)

    def finalize(acc, qc):
        inv_l = 1.0 / acc[V_HEAD:V_HEAD + 1]
        out = (acc[:V_HEAD] * inv_l).T
        o_ref[0, pl.ds(pl.multiple_of(qc * q_chunk, q_chunk), q_chunk), :] = out.astype(o_ref.dtype)

    def k_group(base, qc, q_next, carry, first, last):
        m, acc, s_max = carry
        for t in range(unroll):
            final = last and t == unroll - 1
            if not final:
                next_max = put_scores((t + 1) % 2, base + t + 1, qc)
            elif q_next is not None:
                next_max = put_scores(0, 0, q_next)
            else:
                next_max = s_max
            if first and t == 0 and n_qc > 1:
                finalize(acc_ref[...], jnp.maximum(qc - 1, 0))
            if last and t == 0:
                s_meta = jnp.where(meta_valid, _dot(km_ref[0, 0], q_ref[0, 0, qc]), NEG_BIG)
                sm_ref[...] = s_meta
                meta_max = jnp.max(s_meta, axis=0, keepdims=True)
            m, acc = accumulate(s_ref[t % 2], s_max, v_ref[0, 0, base + t], m, acc)
            if last and t == 1:
                m, acc = accumulate(sm_ref[...], meta_max, vm_ref[0, 0], m, acc)
            s_max = next_max
        return m, acc, s_max

    def q_chunk_body(qc, q_next, s_max):
        m0 = jnp.full((1, q_chunk), NEG_BIG, F32)
        acc0 = jnp.zeros((V_HEAD + BF16_SUBLANES, q_chunk), F32)
        carry = (m0, acc0, s_max)
        if n_groups > 1:
            carry = k_group(0, qc, q_next, carry, True, False)
        if n_groups > 2:
            carry = lax.fori_loop(
                1, n_groups - 1, lambda g, c: k_group(g * unroll, qc, q_next, c, False, False), carry)
        _, acc, s_max = k_group((n_groups - 1) * unroll, qc, q_next, carry, n_groups == 1, True)
        if n_qc == 1:
            finalize(acc, qc)
        else:
            acc_ref[...] = acc
        return s_max

    s_max = put_scores(0, 0, 0)
    if n_qc == 1:
        q_chunk_body(0, None, s_max)
    else:
        acc_ref[...] = jnp.ones(acc_ref.shape, F32)
        chunks_per_trip = 2 if n_qc % 2 == 0 and n_qc > 2 else 1

        def q_trip(i, s_max):
            for c in range(chunks_per_trip):
                qc = chunks_per_trip * i + c
                s_max = q_chunk_body(qc, jnp.minimum(qc + 1, n_qc - 1), s_max)
            return s_max

        lax.fori_loop(0, n_qc // chunks_per_trip, q_trip, s_max)
        finalize(acc_ref[...], n_qc - 1)


def _attention(q_t, k, v_t, k_meta, v_meta_t, *, q_chunks_per_step):
    b, _, n_q_total, _, q_chunk = q_t.shape
    seq_k = k.shape[2]
    n_kc = seq_k // K_CHUNK
    unroll = min(K_UNROLL, n_kc)
    assert unroll % 2 == 0 and n_kc % unroll == 0
    n_qc = q_chunks_per_step
    n_steps = n_q_total // n_qc
    tq = n_qc * q_chunk
    kern = functools.partial(_attn_kernel, n_qc=n_qc, n_kc=n_kc, q_chunk=q_chunk, unroll=unroll)
    return pl.pallas_call(
        kern,
        grid=(b, HEADS, n_steps),
        in_specs=[
            pl.BlockSpec((1, 1, n_qc, QK_PAD, q_chunk), lambda i, h, j: (i, h, j, 0, 0)),
            pl.BlockSpec((1, 1, seq_k, QK_PAD), lambda i, h, j: (i, h, 0, 0)),
            pl.BlockSpec((1, 1, n_kc, V_HEAD, K_CHUNK), lambda i, h, j: (i, h, 0, 0, 0)),
            pl.BlockSpec((1, 1, META_PAD, QK_PAD), lambda i, h, j: (i, h, 0, 0)),
            pl.BlockSpec((1, 1, V_HEAD, META_PAD), lambda i, h, j: (i, h, 0, 0)),
        ],
        out_specs=pl.BlockSpec((1, tq, V_HEAD), lambda i, h, j: (i, j, h)),
        out_shape=jax.ShapeDtypeStruct((b, n_q_total * q_chunk, MLA_WIDTH), BF16),
        scratch_shapes=[pltpu.VMEM((2, K_CHUNK, q_chunk), F32), pltpu.VMEM((META_PAD, q_chunk), F32),
                        pltpu.VMEM((V_HEAD + BF16_SUBLANES, q_chunk), F32)],
        compiler_params=pltpu.CompilerParams(
            dimension_semantics=("parallel", "parallel", "parallel"), vmem_limit_bytes=VMEM_LIMIT),
        name="attention",
    )(q_t, k, v_t, k_meta, v_meta_t)


def _conv_body(g_b, u_prev, u_cur, u_next, conv_w):
    y = u_prev * conv_w[0:1] + u_cur * conv_w[1:2] + u_next * conv_w[2:3]
    return (g_b.astype(F32) * y).astype(BF16)


def _mix_body(x, attn, conv, w_out_ref, g_post_mix, g_pre_ffn):
    mix = _dot(attn, w_out_ref[:MLA_WIDTH, :]) + _dot(conv, w_out_ref[MLA_WIDTH:, :])
    x1 = x + _rms(mix, g_post_mix)
    return x1, _rms(x1, g_pre_ffn).astype(BF16)


def _ffn_chunk(h, cols, w_gate_ref, w_up_ref, w_down_ref):
    c0, c1 = cols
    gate = _dot(h, w_gate_ref[:, c0:c1])
    up = _dot(h, w_up_ref[:, c0:c1])
    act = (gate * (0.5 + 0.5 * jnp.tanh(0.5 * gate)) * up).astype(BF16)
    return _dot(act, w_down_ref[c0:c1, :])


def _mix_ffn_body(x, attn, g_b, u_prev, u_cur, u_next, conv_w, w_out_ref, g_post_mix, g_pre_ffn,
                  w_gate_ref, w_up_ref, w_down_ref, g_post_ffn):
    conv = _conv_body(g_b, u_prev, u_cur, u_next, conv_w)
    x1, h = _mix_body(x, attn, conv, w_out_ref, g_post_mix, g_pre_ffn)
    first_cols, second_cols = FF_CHUNKS
    f = (_ffn_chunk(h, first_cols, w_gate_ref, w_up_ref, w_down_ref)
         + _ffn_chunk(h, second_cols, w_gate_ref, w_up_ref, w_down_ref))
    return x1 + _rms(f, g_post_ffn)


def _conv_neighbours(u_ref, u_before_ref, u_after_ref, u_meta_ref, j, tiles_per_seq):
    u_cur = u_ref[0].astype(F32)
    tail = BF16_SUBLANES - 1
    before = jnp.where(j == 0, u_meta_ref[0, tail:tail + 1, :], u_before_ref[0, tail:tail + 1, :])
    after = jnp.where(j == tiles_per_seq - 1, jnp.zeros_like(u_after_ref[0, 0:1, :]),
                      u_after_ref[0, 0:1, :])
    row = lax.broadcasted_iota(jnp.int32, u_cur.shape, 0)
    u_prev = jnp.where(row == 0, before.astype(F32), pltpu.roll(u_cur, 1, 0))
    u_next = jnp.where(row == FFN_TILE - 1, after.astype(F32), pltpu.roll(u_cur, FFN_TILE - 1, 0))
    return u_prev, u_cur, u_next


def _mix_ffn_kernel(*refs, tiles_per_seq, n_tiles):
    n_in = 7
    first, tile_b, tile_a = refs[0:n_in], refs[n_in:2 * n_in], refs[2 * n_in:3 * n_in]
    (conv_w_ref, w_out_ref, g_post_mix_ref, g_pre_ffn_ref, w_gate_ref, w_up_ref, w_down_ref,
     g_post_ffn_ref) = refs[3 * n_in:3 * n_in + 8]
    o_ref, x1_a, h_a, x1_b, h_b = refs[3 * n_in + 8:]
    s = pl.program_id(0)
    first_cols, second_cols = FF_CHUNKS

    def conv(tile, j):
        _, _, gb_ref, u_ref, u_before_ref, u_after_ref, u_meta_ref = tile
        u_prev, u_cur, u_next = _conv_neighbours(u_ref, u_before_ref, u_after_ref, u_meta_ref, j,
                                                 tiles_per_seq)
        return _conv_body(gb_ref[0], u_prev, u_cur, u_next, conv_w_ref[...])

    def mix(tile, conv_out, x1_scr, h_scr):
        x1, h = _mix_body(tile[0][0], tile[1][0], conv_out, w_out_ref, g_post_mix_ref[...],
                          g_pre_ffn_ref[...])
        x1_scr[...] = x1
        h_scr[...] = h

    def ffn_chunk(h_scr, cols):
        return _ffn_chunk(h_scr[...], cols, w_gate_ref, w_up_ref, w_down_ref)

    @pl.when(s == 0)
    def _():
        mix(first, conv(first, 0), x1_a, h_a)

    conv_b = conv(tile_b, lax.rem(2 * s + 1, tiles_per_seq))
    f_a = ffn_chunk(h_a, first_cols)
    mix(tile_b, conv_b, x1_b, h_b)
    f_a = f_a + ffn_chunk(h_a, second_cols)
    o_ref[0, :FFN_TILE, :] = x1_a[...] + _rms(f_a, g_post_ffn_ref[...])

    conv_a = conv(tile_a, lax.rem(jnp.minimum(2 * s + 2, n_tiles - 1), tiles_per_seq))
    f_b = ffn_chunk(h_b, first_cols)
    mix(tile_a, conv_a, x1_a, h_a)
    f_b = f_b + ffn_chunk(h_b, second_cols)
    o_ref[0, FFN_TILE:, :] = x1_b[...] + _rms(f_b, g_post_ffn_ref[...])


def _mix_ffn_meta_kernel(x_ref, attn_ref, gb_ref, u_prev_ref, u_ref, u_next_ref,
                         conv_w_ref, w_out_ref, g_post_mix_ref, g_pre_ffn_ref,
                         w_gate_ref, w_up_ref, w_down_ref, g_post_ffn_ref, o_ref):
    o_ref[...] = _mix_ffn_body(
        x_ref[...], attn_ref[...], gb_ref[...], u_prev_ref[...].astype(F32),
        u_ref[...].astype(F32), u_next_ref[...].astype(F32), conv_w_ref[...], w_out_ref,
        g_post_mix_ref[...], g_pre_ffn_ref[...], w_gate_ref, w_up_ref, w_down_ref,
        g_post_ffn_ref[...])


def _ffn_weight_specs(layer):
    return [
        _layer_spec((CONV_K, CONV_WIDTH), layer),
        _layer_spec((D_MODEL, D_MODEL), layer),
        _layer_spec((1, D_MODEL), layer),
        _layer_spec((1, D_MODEL), layer),
        _layer_spec((D_MODEL, D_FF), layer),
        _layer_spec((D_MODEL, D_FF), layer),
        _layer_spec((D_FF, D_MODEL), layer),
        _layer_spec((1, D_MODEL), layer),
    ]


def _ffn_weights(lw):
    return (lw["conv_w"], lw["w_out"], lw["g_post_mix"], lw["g_pre_ffn"], lw["w_gate"], lw["w_up"],
            lw["w_down"], lw["g_post_ffn"])


def _mix_ffn_real(x, attn, g_b, u, u_meta, lw, layer):
    b, seq, _ = x.shape
    tiles_per_seq = seq // FFN_TILE
    n_tiles = b * tiles_per_seq
    halo_per_tile = FFN_TILE // BF16_SUBLANES
    n_halo = seq // BF16_SUBLANES

    def tile_specs(tile_of_step, **kw):
        def at(fn):
            def index_map(s):
                q = tile_of_step(s)
                return fn(q // tiles_per_seq, q % tiles_per_seq)
            return index_map
        rows = lambda w: pl.BlockSpec((1, FFN_TILE, w), at(lambda i, j: (i, j, 0)), **kw)
        halo = lambda fn: pl.BlockSpec((1, BF16_SUBLANES, CONV_WIDTH), at(fn), **kw)
        return [rows(D_MODEL), rows(MLA_WIDTH), rows(CONV_WIDTH), rows(CONV_WIDTH),
                halo(lambda i, j: (i, jnp.maximum(j * halo_per_tile - 1, 0), 0)),
                halo(lambda i, j: (i, jnp.minimum((j + 1) * halo_per_tile, n_halo - 1), 0)),
                halo(lambda i, j: (i, 0, 0))]

    in_specs = (tile_specs(lambda s: 0 * s, pipeline_mode=pl.Buffered(1))
                + tile_specs(lambda s: 2 * s + 1)
                + tile_specs(lambda s: jnp.minimum(2 * s + 2, n_tiles - 1))
                + _ffn_weight_specs(layer))
    steps_per_seq = tiles_per_seq // 2
    kern = functools.partial(_mix_ffn_kernel, tiles_per_seq=tiles_per_seq, n_tiles=n_tiles)
    acts = (x, attn, g_b, u, u, u, u_meta)
    return pl.pallas_call(
        kern,
        grid=(n_tiles // 2,),
        in_specs=in_specs,
        out_specs=pl.BlockSpec((1, 2 * FFN_TILE, D_MODEL),
                               lambda s: (s // steps_per_seq, s % steps_per_seq, 0)),
        out_shape=jax.ShapeDtypeStruct((b, seq, D_MODEL), F32),
        scratch_shapes=[pltpu.VMEM((FFN_TILE, D_MODEL), F32), pltpu.VMEM((FFN_TILE, D_MODEL), BF16),
                        pltpu.VMEM((FFN_TILE, D_MODEL), F32), pltpu.VMEM((FFN_TILE, D_MODEL), BF16)],
        compiler_params=pltpu.CompilerParams(
            dimension_semantics=("arbitrary",), vmem_limit_bytes=VMEM_LIMIT),
        name="mix_ffn_real",
    )(*acts, *acts, *acts, *_ffn_weights(lw))


def _mix_ffn_meta(xm, attn, g_b, u_prev, u, u_next, lw, layer):
    r = META_ROWS
    in_specs = [_const_spec((r, D_MODEL)), _const_spec((r, MLA_WIDTH))] \
        + [_const_spec((r, CONV_WIDTH)) for _ in range(4)] + _ffn_weight_specs(layer)
    return pl.pallas_call(
        _mix_ffn_meta_kernel,
        grid=(1,),
        in_specs=in_specs,
        out_specs=pl.BlockSpec((r, D_MODEL), lambda i: (0, 0)),
        out_shape=jax.ShapeDtypeStruct((r, D_MODEL), F32),
        compiler_params=pltpu.CompilerParams(
            dimension_semantics=("arbitrary",), vmem_limit_bytes=VMEM_LIMIT),
        name="mix_ffn_meta",
    )(xm, attn, g_b, u_prev, u, u_next, *_ffn_weights(lw))


def _pack_weights(pre_mix_g, w_in, q_norm_g, w_q_up, kv_norm_g, w_kv_up, conv_w, w_out,
                  post_mix_g, pre_ffn_g, w_gate, w_up, w_down, post_ffn_g):
    i2 = Q_LORA + KV_LORA + QK_ROPE
    zpad = jnp.zeros((DEPTH, D_MODEL, LANES - QK_ROPE), F32)
    w_in_p = jnp.concatenate([w_in[:, :, :i2], zpad, w_in[:, :, i2:]], axis=2).astype(BF16)
    wq_t = jnp.swapaxes(w_q_up, 1, 2).astype(BF16)
    wkv = w_kv_up.reshape(DEPTH, KV_LORA, HEADS, QK_NOPE + V_HEAD)
    wk = wkv[..., :QK_NOPE].reshape(DEPTH, KV_LORA, HEADS * QK_NOPE).astype(BF16)
    wv_t = jnp.swapaxes(wkv[..., QK_NOPE:].reshape(DEPTH, KV_LORA, HEADS * V_HEAD), 1, 2).astype(BF16)
    row = lambda g: g.reshape(DEPTH, 1, -1).astype(F32)
    return dict(
        g_pre=row(pre_mix_g), w_in=w_in_p, g_q=row(q_norm_g), wq_t=wq_t, g_kv=row(kv_norm_g),
        wk=wk, wv_t=wv_t, conv_w=conv_w.astype(F32), w_out=w_out.astype(BF16),
        g_post_mix=row(post_mix_g), g_pre_ffn=row(pre_ffn_g), w_gate=w_gate.astype(BF16),
        w_up=w_up.astype(BF16), w_down=w_down.astype(BF16), g_post_ffn=row(post_ffn_g))


def _rope_tables(pos):
    inv_freq = np.float32(ROPE_THETA) ** (-np.arange(0, QK_ROPE, 2, dtype=np.float32) / np.float32(QK_ROPE))
    ang = (pos.astype(np.float32)[:, None] * inv_freq.astype(np.float32)[None, :]).astype(np.float64)
    c, s = np.cos(ang).astype(np.float32), np.sin(ang).astype(np.float32)
    z = np.zeros((pos.shape[0], LANES - QK_ROPE), np.float32)
    cos = np.concatenate([c, c, z], axis=1)
    sin = np.concatenate([-s, s, z], axis=1)
    return tuple(jnp.asarray(t) for t in (cos, sin, np.ascontiguousarray(c.T), np.ascontiguousarray(s.T)))


def _query_chunk(seq):
    return 1024 if seq // K_CHUNK <= K_UNROLL else 512


def _meta_to_groups(a, batches, axis):
    out, r0 = [], 0
    for b in batches:
        sl = lax.slice_in_dim(a, r0, r0 + b * N_META, axis=axis)
        shape = a.shape[:axis] + (b, N_META) + a.shape[axis + 1:]
        out.append(sl.reshape(shape))
        r0 += b * N_META
    return out


def _join_meta(parts):
    flat = [p.reshape(-1, p.shape[-1]) for p in parts]
    rows = sum(f.shape[0] for f in flat)
    flat.append(jnp.zeros((META_ROWS - rows, flat[0].shape[-1]), flat[0].dtype))
    return jnp.concatenate(flat, axis=0)


def kernel(x_prompt, x_sample, meta_tokens, pre_mix_g, w_in, q_norm_g, w_q_up, kv_norm_g, w_kv_up,
           conv_w, w_out, post_mix_g, pre_ffn_g, w_gate, w_up, w_down, post_ffn_g):
    xs = [x_prompt, x_sample]
    batches = [x.shape[0] for x in xs]
    assert sum(batches) * N_META <= META_ROWS
    for x in xs:
        assert x.shape[1] % max(ROW_TILE, 2 * FFN_TILE) == 0 and x.shape[2] == D_MODEL

    real_tabs = [_rope_tables(N_META + np.arange(x.shape[1])) for x in xs]
    meta_pos = np.arange(META_ROWS) % N_META
    meta_tabs = _rope_tables(meta_pos)

    xm = _join_meta([jnp.broadcast_to(meta_tokens.astype(F32)[None], (b, N_META, D_MODEL))
                     for b in batches])

    lw = _pack_weights(pre_mix_g, w_in, q_norm_g, w_q_up, kv_norm_g, w_kv_up, conv_w, w_out,
                       post_mix_g, pre_ffn_g, w_gate, w_up, w_down, post_ffn_g)
    for layer in range(DEPTH):
        qm, km, vm, gbm, um = _proj_meta(xm, lw, layer, meta_tabs)
        qm_g = _meta_to_groups(qm, batches, 2)
        km_g = _meta_to_groups(km, batches, 1)
        vm_g = _meta_to_groups(vm, batches, 2)
        um_g = _meta_to_groups(um, batches, 0)

        new_xs, attn_meta, u_prev_meta, u_next_meta = [], [], [], []
        for g, x in enumerate(xs):
            q_chunk = _query_chunk(x.shape[1])
            q_t, k, v_t, g_b, u = _proj_real(x, lw, layer, real_tabs[g], q_chunk)
            k_meta = jnp.pad(jnp.transpose(km_g[g], (1, 0, 2, 3)),
                             ((0, 0), (0, 0), (0, META_PAD - N_META), (0, 0)))
            v_meta_t = jnp.pad(jnp.transpose(vm_g[g], (2, 0, 1, 3)),
                               ((0, 0), (0, 0), (0, 0), (0, META_PAD - N_META)))
            q_meta_t = jnp.pad(jnp.transpose(qm_g[g], (2, 0, 1, 3)),
                               ((0, 0), (0, 0), (0, 0), (0, META_PAD - N_META)))[:, :, None]
            attn = _attention(q_t, k, v_t, k_meta, v_meta_t,
                              q_chunks_per_step=min(x.shape[1], Q_PER_STEP) // q_chunk)
            attn_m = _attention(q_meta_t, k, v_t, k_meta, v_meta_t, q_chunks_per_step=1)
            attn_meta.append(attn_m[:, :N_META])
            um_b = um_g[g]
            zero_row = jnp.zeros_like(um_b[:, :1])
            u_prev_meta.append(jnp.concatenate([zero_row, um_b[:, :-1]], axis=1))
            u_next_meta.append(jnp.concatenate([um_b[:, 1:], u[:, :1]], axis=1))
            new_xs.append(_mix_ffn_real(x, attn, g_b, u, um_b, lw, layer))

        xm = _mix_ffn_meta(xm, _join_meta(attn_meta), gbm, _join_meta(u_prev_meta), um,
                           _join_meta(u_next_meta), lw, layer)
        xs = new_xs

    return (xs[0], xs[1])
```
